```python
import jax, jax.numpy as jnp
from jax import lax
import numpy as np

D_MODEL = 2048
BATCH = 4
SEQ = 8192
DEPTH = 1

ATTN_HEADS = 16
HEAD_DIM = 128
ATTN_WIDTH = ATTN_HEADS * HEAD_DIM
MOBA_BLOCK = 256
MOBA_TOPK = 3
MOBA_QCHUNK = 128
LRU_WIDTH = D_MODEL
LRU_BLOCKS = 8
LRU_BLOCK_WIDTH = LRU_WIDTH // LRU_BLOCKS
CONV_WIDTH = 4
LRU_C = 8.0
N_BRANCHES = 2
IN_COLS = 3 * ATTN_WIDTH + 2 * LRU_WIDTH + N_BRANCHES * D_MODEL
N_EXPERTS = 32
MOE_TOPK = 4
D_FF = D_MODEL
SWIGLU_ALPHA = 1.702
SWIGLU_LIMIT = 7.0
MOE_BLOCK = 512
RMS_EPS = 1e-6

kernel_name = "moba_rglru_gated_hybrid_moe"


def rms_norm(x, g):
    xf = x.astype(jnp.float32)
    y = xf * lax.rsqrt(jnp.mean(xf * xf, axis=-1, keepdims=True) + RMS_EPS)
    return (y * g.astype(jnp.float32)).astype(x.dtype)


def group_rows(ids, num_groups, blk):
    n = ids.shape[0]
    p = (-(-n // blk)) * blk + num_groups * blk
    counts = jnp.bincount(ids, length=num_groups).astype(jnp.int32)
    padded = (counts + blk - 1) // blk * blk
    pad_end = jnp.cumsum(padded)
    pad_start = pad_end - padded
    start = jnp.cumsum(counts) - counts
    order = jnp.argsort(ids)
    sorted_ids = ids[order]
    rank = jnp.arange(n, dtype=jnp.int32) - start[sorted_ids]
    dest = jnp.zeros((n,), jnp.int32).at[order].set(pad_start[sorted_ids] + rank)
    row_starts = jnp.arange(p // blk, dtype=jnp.int32) * blk
    block_group = jnp.minimum(jnp.searchsorted(pad_end, row_starts, side='right'), num_groups - 1).astype(jnp.int32)
    return dest, block_group, p


def moba_head(q, k, v):
    out_dtype = q.dtype
    q, k, v = q.astype(jnp.float32), k.astype(jnp.float32), v.astype(jnp.float32)
    sp = q.shape[0]
    nb = sp // MOBA_BLOCK
    scale = HEAD_DIM ** -0.5
    qblk = jnp.arange(sp, dtype=jnp.int32) // MOBA_BLOCK
    kb = k.reshape(nb, MOBA_BLOCK, HEAD_DIM)
    vb = v.reshape(nb, MOBA_BLOCK, HEAD_DIM)
    qb = q.reshape(nb, MOBA_BLOCK, HEAD_DIM)
    s_self = jnp.einsum('nqd,nkd->nqk', qb, kb) * scale
    causal = jnp.tril(jnp.ones((MOBA_BLOCK, MOBA_BLOCK), bool))
    s_self = jnp.where(causal[None], s_self, -jnp.inf)
    m_self = jnp.max(s_self, axis=-1)
    p_self = jnp.exp(s_self - m_self[..., None])
    l_self = jnp.sum(p_self, axis=-1)
    o_self = (jnp.einsum('nqk,nkd->nqd', p_self, vb) / l_self[..., None]).reshape(sp, HEAD_DIM)
    lse_self = (m_self + jnp.log(l_self)).reshape(sp)
    k_mean = jnp.mean(kb, axis=1)
    gate = q @ k_mean.T
    past = jnp.arange(nb, dtype=jnp.int32)[None, :] < qblk[:, None]
    gate = jnp.where(past, gate, -jnp.inf)
    n_sel = min(MOBA_TOPK, nb)
    _, sel = lax.top_k(gate, n_sel)
    valid = sel < qblk[:, None]
    n_asg = sp * n_sel
    dest, block_group, p = group_rows(sel.reshape(n_asg).astype(jnp.int32), nb, MOBA_QCHUNK)
    row_q = jnp.zeros((p,), jnp.int32).at[dest].set(jnp.arange(n_asg, dtype=jnp.int32) // n_sel)
    qg = q[row_q].reshape(p // MOBA_QCHUNK, MOBA_QCHUNK, HEAD_DIM)
    kg = kb[block_group]
    vg = vb[block_group]
    s = jnp.einsum('cqd,ckd->cqk', qg, kg) * scale
    m = jnp.max(s, axis=-1)
    pr = jnp.exp(s - m[..., None])
    l = jnp.sum(pr, axis=-1)
    o = jnp.einsum('cqk,ckd->cqd', pr, vg) / l[..., None]
    o_sel = o.reshape(p, HEAD_DIM)[dest].reshape(sp, n_sel, HEAD_DIM)
    lse_sel = (m + jnp.log(l)).reshape(p)[dest].reshape(sp, n_sel)
    lse_sel = jnp.where(valid, lse_sel, -jnp.inf)
    w = jax.nn.softmax(jnp.concatenate([lse_self[:, None], lse_sel], axis=1), axis=-1)
    o_all = w[:, :1] * o_self + jnp.einsum('sk,skd->sd', w[:, 1:], o_sel)
    return o_all.astype(out_dtype)


def causal_conv(x, w, b):
    y = lax.conv_general_dilated(x, w[:, None, :].astype(x.dtype), window_strides=(1,),
                                 padding=[(CONV_WIDTH - 1, 0)],
                                 dimension_numbers=('NWC', 'WIO', 'NWC'),
                                 feature_group_count=x.shape[-1])
    return y + b.astype(x.dtype)


def rg_lru(x, wa, ba, wx, bx, lam):
    bsz, s, r = x.shape
    xf = x.astype(jnp.float32)
    xb = xf.reshape(bsz, s, LRU_BLOCKS, LRU_BLOCK_WIDTH)
    rt = jax.nn.sigmoid(jnp.einsum('bsnc,ncd->bsnd', xb, wa.astype(jnp.float32)).reshape(bsz, s, r) + ba.astype(jnp.float32))
    it = jax.nn.sigmoid(jnp.einsum('bsnc,ncd->bsnd', xb, wx.astype(jnp.float32)).reshape(bsz, s, r) + bx.astype(jnp.float32))
    log_a = -LRU_C * rt * jax.nn.softplus(-lam.astype(jnp.float32))
    a = jnp.exp(log_a)
    bterm = jnp.sqrt(-jnp.expm1(2.0 * log_a)) * (it * xf)

    def step(h, ab):
        a_t, b_t = ab
        h = a_t * h + b_t
        return h, h

    _, hs = lax.scan(step, jnp.zeros((bsz, r), jnp.float32), (a.transpose(1, 0, 2), bterm.transpose(1, 0, 2)))
    return hs.transpose(1, 0, 2)


def mixer(h, w_in, conv_w, conv_b, lru_wa, lru_ba, lru_wx, lru_bx, lru_lambda, w_attn_branch, w_rnn_branch, w_out):
    bsz, s, _ = h.shape
    sp = (-(-s // MOBA_BLOCK)) * MOBA_BLOCK
    offs = [ATTN_WIDTH, 2 * ATTN_WIDTH, 3 * ATTN_WIDTH, 3 * ATTN_WIDTH + LRU_WIDTH, 3 * ATTN_WIDTH + 2 * LRU_WIDTH]
    w_q, w_k, w_v, w_rx, w_ry, w_g = jnp.split(w_in, offs, axis=1)

    def heads(w):
        t = (h @ w).reshape(bsz, s, ATTN_HEADS, HEAD_DIM).transpose(0, 2, 1, 3)
        t = jnp.pad(t, ((0, 0), (0, 0), (0, sp - s), (0, 0)))
        return t.reshape(bsz * ATTN_HEADS, sp, HEAD_DIM)

    o = lax.map(lambda qkv: moba_head(*qkv), (heads(w_q), heads(w_k), heads(w_v)))
    o = o.reshape(bsz, ATTN_HEADS, sp, HEAD_DIM)[:, :, :s].transpose(0, 2, 1, 3).reshape(bsz, s, ATTN_WIDTH)
    y_attn = o.astype(h.dtype) @ w_attn_branch

    xr = causal_conv(h @ w_rx, conv_w, conv_b)
    yr = jax.nn.gelu(h @ w_ry)
    hr = rg_lru(xr, lru_wa, lru_ba, lru_wx, lru_bx, lru_lambda)
    y_rnn = (hr.astype(h.dtype) * yr) @ w_rnn_branch

    gates = jax.nn.sigmoid((h @ w_g).astype(jnp.float32)).astype(h.dtype)
    g_attn, g_rnn = jnp.split(gates, N_BRANCHES, axis=-1)
    return (g_attn * y_attn + g_rnn * y_rnn) @ w_out


def moe(h, router_w, router_b, w_gate, b_gate, w_up, b_up, w_down, b_down):
    bsz, s, d = h.shape
    t = bsz * s
    xf = h.reshape(t, d)
    logits = (xf @ router_w + router_b).astype(jnp.float32)
    top_val, top_idx = lax.top_k(logits, MOE_TOPK)
    gates = jax.nn.softmax(top_val, axis=-1)
    n_asg = t * MOE_TOPK
    dest, block_group, p = group_rows(top_idx.reshape(n_asg).astype(jnp.int32), N_EXPERTS, MOE_BLOCK)
    row_tok = jnp.zeros((p,), jnp.int32).at[dest].set(jnp.arange(n_asg, dtype=jnp.int32) // MOE_TOPK)
    row_gate = jnp.zeros((p,), h.dtype).at[dest].set(gates.reshape(n_asg).astype(h.dtype))
    nblk = p // MOE_BLOCK

    def expert_block(args):
        e, tok, g = args
        xb = xf[tok]
        glu = jnp.minimum(xb @ w_gate[e] + b_gate[e], SWIGLU_LIMIT)
        lin = jnp.clip(xb @ w_up[e] + b_up[e], -SWIGLU_LIMIT, SWIGLU_LIMIT)
        act = glu * jax.nn.sigmoid(SWIGLU_ALPHA * glu) * (lin + 1.0)
        y = act @ w_down[e] + b_down[e]
        return y * g[:, None]

    y = lax.map(expert_block, (block_group, row_tok.reshape(nblk, MOE_BLOCK), row_gate.reshape(nblk, MOE_BLOCK)))
    out = jnp.zeros((t, d), h.dtype).at[row_tok].add(y.reshape(p, d))
    return out.reshape(bsz, s, d)


def setup_inputs(seed: int = 0) -> dict:
    key = jax.random.key(seed)
    ks = jax.random.split(key, 24)
    f32 = jnp.float32
    nrm = lambda k, shape, scale: jax.random.normal(k, shape, f32) * scale
    u = jax.random.uniform(ks[9], (DEPTH, LRU_WIDTH), f32, 0.9, 0.999)
    base = u ** (1.0 / LRU_C)
    lam = jnp.log(base) - jnp.log1p(-base)
    return {
        "x": nrm(ks[0], (BATCH, SEQ, D_MODEL), 1.0),
        "norm_mix_g": 1.0 + nrm(ks[1], (DEPTH, D_MODEL), 0.02),
        "w_in": nrm(ks[2], (DEPTH, D_MODEL, IN_COLS), D_MODEL ** -0.5),
        "conv_w": nrm(ks[3], (DEPTH, CONV_WIDTH, LRU_WIDTH), CONV_WIDTH ** -0.5),
        "conv_b": nrm(ks[4], (DEPTH, LRU_WIDTH), 0.01),
        "lru_wa": nrm(ks[5], (DEPTH, LRU_BLOCKS, LRU_BLOCK_WIDTH, LRU_BLOCK_WIDTH), LRU_BLOCK_WIDTH ** -0.5),
        "lru_ba": nrm(ks[6], (DEPTH, LRU_WIDTH), 0.01),
        "lru_wx": nrm(ks[7], (DEPTH, LRU_BLOCKS, LRU_BLOCK_WIDTH, LRU_BLOCK_WIDTH), LRU_BLOCK_WIDTH ** -0.5),
        "lru_bx": nrm(ks[8], (DEPTH, LRU_WIDTH), 0.01),
        "lru_lambda": lam,
        "w_attn_branch": nrm(ks[10], (DEPTH, ATTN_WIDTH, D_MODEL), ATTN_WIDTH ** -0.5),
        "w_rnn_branch": nrm(ks[11], (DEPTH, LRU_WIDTH, D_MODEL), LRU_WIDTH ** -0.5),
        "w_out": nrm(ks[12], (DEPTH, D_MODEL, D_MODEL), D_MODEL ** -0.5),
        "norm_ffn_g": 1.0 + nrm(ks[13], (DEPTH, D_MODEL), 0.02),
        "router_w": nrm(ks[14], (DEPTH, D_MODEL, N_EXPERTS), D_MODEL ** -0.5),
        "router_b": nrm(ks[15], (DEPTH, N_EXPERTS), 0.01),
        "w_gate": nrm(ks[16], (DEPTH, N_EXPERTS, D_MODEL, D_FF), D_MODEL ** -0.5),
        "b_gate": nrm(ks[17], (DEPTH, N_EXPERTS, D_FF), 0.01),
        "w_up": nrm(ks[18], (DEPTH, N_EXPERTS, D_MODEL, D_FF), D_MODEL ** -0.5),
        "b_up": nrm(ks[19], (DEPTH, N_EXPERTS, D_FF), 0.01),
        "w_down": nrm(ks[20], (DEPTH, N_EXPERTS, D_FF, D_MODEL), D_FF ** -0.5),
        "b_down": nrm(ks[21], (DEPTH, N_EXPERTS, D_MODEL), 0.01),
        "norm_final_g": 1.0 + nrm(ks[22], (D_MODEL,), 0.02),
    }


def reference(x, norm_mix_g, w_in, conv_w, conv_b, lru_wa, lru_ba, lru_wx, lru_bx, lru_lambda,
              w_attn_branch, w_rnn_branch, w_out, norm_ffn_g, router_w, router_b,
              w_gate, b_gate, w_up, b_up, w_down, b_down, norm_final_g):
    for l in range(DEPTH):
        hm = rms_norm(x, norm_mix_g[l])
        x = x + mixer(hm, w_in[l], conv_w[l], conv_b[l], lru_wa[l], lru_ba[l], lru_wx[l], lru_bx[l],
                      lru_lambda[l], w_attn_branch[l], w_rnn_branch[l], w_out[l])
        hf = rms_norm(x, norm_ffn_g[l])
        x = x + moe(hf, router_w[l], router_b[l], w_gate[l], b_gate[l], w_up[l], b_up[l], w_down[l], b_down[l])
    return rms_norm(x, norm_final_g)
```

```python
import functools

import jax
import jax.numpy as jnp
from jax import lax
from jax.experimental import pallas as pl
from jax.experimental.pallas import tpu as pltpu

ATTN_HEADS = 16
HEAD_DIM = 128
MOBA_BLOCK = 256
MOBA_TOPK = 3
LRU_BLOCKS = 8
CONV_WIDTH = 4
LRU_C = 8.0
MOE_TOPK = 4
SWIGLU_ALPHA = 1.702
SWIGLU_LIMIT = 7.0
MOE_BLOCK = 512
RMS_EPS = 1e-6

LANES = 128
SUBLANES = 8
VMEM_LIMIT = 56 * 1024 * 1024
NEG_BIG = -1e30

BF16 = jnp.bfloat16
F32 = jnp.float32


def _cparams(sem):
    return pltpu.CompilerParams(dimension_semantics=sem, vmem_limit_bytes=VMEM_LIMIT)


def _sigmoid(x):
    return 1.0 / (1.0 + jnp.exp(-x))


def _gelu_tanh(x):
    c = 0.7978845608028654
    return 0.5 * x * (1.0 + jnp.tanh(c * (x + 0.044715 * (x * x * x))))


def _rmsnorm_kernel(x_ref, g_ref, o_ref):
    x = x_ref[...]
    ms = jnp.mean(x * x, axis=-1, keepdims=True)
    o_ref[...] = (x * lax.rsqrt(ms + RMS_EPS) * g_ref[...]).astype(o_ref.dtype)


def _rmsnorm(x, g, out_dtype, tm=1024):
    t, d = x.shape
    return pl.pallas_call(
        _rmsnorm_kernel,
        out_shape=jax.ShapeDtypeStruct((t, d), out_dtype),
        grid=(t // tm,),
        in_specs=[pl.BlockSpec((tm, d), lambda i: (i, 0)),
                  pl.BlockSpec((1, d), lambda i: (0, 0))],
        out_specs=pl.BlockSpec((tm, d), lambda i: (i, 0)),
        compiler_params=_cparams(("parallel",)),
        name="rmsnorm",
    )(x, g.reshape(1, d).astype(F32))


def _add_rmsnorm_kernel(x_ref, y_ref, g_ref, o_ref):
    x = x_ref[...] + y_ref[...]
    ms = jnp.mean(x * x, axis=-1, keepdims=True)
    o_ref[...] = (x * lax.rsqrt(ms + RMS_EPS) * g_ref[...]).astype(o_ref.dtype)


def _add_rmsnorm(x, y, g, tm=512):
    t, d = x.shape
    return pl.pallas_call(
        _add_rmsnorm_kernel,
        out_shape=jax.ShapeDtypeStruct((t, d), F32),
        grid=(t // tm,),
        in_specs=[pl.BlockSpec((tm, d), lambda i: (i, 0)),
                  pl.BlockSpec((tm, d), lambda i: (i, 0)),
                  pl.BlockSpec((1, d), lambda i: (0, 0))],
        out_specs=pl.BlockSpec((tm, d), lambda i: (i, 0)),
        compiler_params=_cparams(("parallel",)),
        name="add_rmsnorm",
    )(x, y, g.reshape(1, d).astype(F32))


def _proj_kernel(a_ref, w_ref, o_ref, *, epilogue, n_scaled_tiles, scale):
    acc = jnp.dot(a_ref[...], w_ref[...], preferred_element_type=F32)
    if epilogue == "scale_head":
        s = jnp.where(pl.program_id(1) < n_scaled_tiles, scale, 1.0).astype(F32)
        acc = acc * s
    elif epilogue == "gelu":
        acc = _gelu_tanh(acc)
    elif epilogue == "sigmoid":
        acc = _sigmoid(acc)
    o_ref[...] = acc.astype(o_ref.dtype)


def _proj(a, w, col0, n, out_dtype, epilogue="none", n_scaled_tiles=0, scale=1.0,
          tm=1024, tn=1024):
    t, k = a.shape
    c0 = col0 // tn
    kern = functools.partial(_proj_kernel, epilogue=epilogue,
                             n_scaled_tiles=n_scaled_tiles, scale=scale)
    return pl.pallas_call(
        kern,
        out_shape=jax.ShapeDtypeStruct((t, n), out_dtype),
        grid=(t // tm, n // tn),
        in_specs=[pl.BlockSpec((tm, k), lambda i, j: (i, 0)),
                  pl.BlockSpec((k, tn), lambda i, j: (0, c0 + j))],
        out_specs=pl.BlockSpec((tm, tn), lambda i, j: (i, j)),
        compiler_params=_cparams(("parallel", "parallel")),
        name="proj_" + epilogue,
    )(a, w)


def _dot_nt(a, b):
    return lax.dot_general(a, b, (((1,), (1,)), ((), ())), preferred_element_type=F32)


def _moba_kernel(q_ref, k_ref, v_ref, o_ref, kext_ref, *, seq):
    blk = MOBA_BLOCK
    nb = seq // blk

    kext_ref[:, :HEAD_DIM] = k_ref[...]
    r_blk = lax.broadcasted_iota(jnp.int32, (seq, LANES), 0) // blk
    c_id = lax.broadcasted_iota(jnp.int32, (seq, LANES), 1)
    kext_ref[:, HEAD_DIM:] = jnp.where(r_blk == c_id, 1.0, 0.0).astype(BF16)

    a_row = lax.broadcasted_iota(jnp.int32, (LANES, seq), 0)
    a_col = lax.broadcasted_iota(jnp.int32, (LANES, seq), 1) // blk
    avg = jnp.where(a_row == a_col, 1.0 / blk, 0.0).astype(BF16)
    kmean = jnp.dot(avg, k_ref[...], preferred_element_type=F32).astype(BF16)

    col = lax.broadcasted_iota(jnp.int32, (blk, LANES), 1)
    tri_r = lax.broadcasted_iota(jnp.int32, (blk, blk), 0)
    tri_c = lax.broadcasted_iota(jnp.int32, (blk, blk), 1)

    def q_block(i, _):
        row0 = pl.multiple_of(i * blk, blk)
        q = q_ref[pl.ds(row0, blk), :]
        gate = _dot_nt(q, kmean)
        past = col < i
        g = jnp.where(past, gate, -jnp.inf)
        sel = jnp.zeros((blk, LANES), jnp.bool_)
        for _k in range(MOBA_TOPK):
            mx = jnp.max(g, axis=1, keepdims=True)
            first = jnp.min(jnp.where(g == mx, col, LANES), axis=1, keepdims=True)
            pick = col == first
            sel = jnp.logical_or(sel, pick)
            g = jnp.where(pick, -jnp.inf, g)
        bias = jnp.where(jnp.logical_and(sel, past), 0.0, NEG_BIG).astype(BF16)
        qext = jnp.concatenate([q, bias], axis=1)

        k_self = k_ref[pl.ds(row0, blk), :]
        v_self = v_ref[pl.ds(row0, blk), :]
        s = _dot_nt(q, k_self)
        s = jnp.where(tri_r >= tri_c, s, NEG_BIG)
        m = jnp.max(s, axis=1, keepdims=True)
        p = jnp.exp(s - m)
        l = jnp.sum(p, axis=1, keepdims=True)
        acc = jnp.dot(p.astype(BF16), v_self, preferred_element_type=F32)

        def kv_block(j, carry):
            m, l, acc = carry
            c0 = pl.multiple_of(j * blk, blk)
            s = _dot_nt(qext, kext_ref[pl.ds(c0, blk), :])
            m_new = jnp.maximum(m, jnp.max(s, axis=1, keepdims=True))
            alpha = jnp.exp(m - m_new)
            p = jnp.exp(s - m_new)
            l = alpha * l + jnp.sum(p, axis=1, keepdims=True)
            acc = alpha * acc + jnp.dot(p.astype(BF16), v_ref[pl.ds(c0, blk), :],
                                        preferred_element_type=F32)
            return m_new, l, acc

        m, l, acc = lax.fori_loop(0, i, kv_block, (m, l, acc))
        o_ref[pl.ds(row0, blk), :] = (acc / l).astype(o_ref.dtype)
        return 0

    lax.fori_loop(0, nb, q_block, 0)


def _moba(qkv, bsz, seq):
    h = ATTN_HEADS
    kern = functools.partial(_moba_kernel, seq=seq)
    return pl.pallas_call(
        kern,
        out_shape=jax.ShapeDtypeStruct((bsz, seq, h * HEAD_DIM), BF16),
        grid=(bsz, h),
        in_specs=[pl.BlockSpec((None, seq, HEAD_DIM), lambda b, hh: (b, 0, hh)),
                  pl.BlockSpec((None, seq, HEAD_DIM), lambda b, hh: (b, 0, h + hh)),
                  pl.BlockSpec((None, seq, HEAD_DIM), lambda b, hh: (b, 0, 2 * h + hh))],
        out_specs=pl.BlockSpec((None, seq, HEAD_DIM), lambda b, hh: (b, 0, hh)),
        scratch_shapes=[pltpu.VMEM((seq, 2 * HEAD_DIM), BF16)],
        compiler_params=_cparams(("parallel", "parallel")),
        name="moba_attention",
    )(qkv, qkv, qkv)


def _lru_kernel(x_ref, y_ref, cw_ref, cb_ref, w_ref, bias_ref, lam_ref, o_ref,
                win_ref, a_ref, b_ref, h_ref, *, ts):
    t_idx = pl.program_id(2)

    @pl.when(t_idx == 0)
    def _():
        win_ref[0:SUBLANES, :] = jnp.zeros((SUBLANES, win_ref.shape[1]), F32)
        h_ref[...] = jnp.zeros_like(h_ref)

    win_ref[SUBLANES:, :] = x_ref[...]
    xc = cb_ref[...] + cw_ref[CONV_WIDTH - 1:CONV_WIDTH, :] * x_ref[...]
    for k in range(CONV_WIDTH - 1):
        shift = CONV_WIDTH - 1 - k
        xc = xc + cw_ref[k:k + 1, :] * win_ref[SUBLANES - shift:SUBLANES - shift + ts, :]
    win_ref[0:SUBLANES, :] = win_ref[ts:ts + SUBLANES, :]

    gates = jnp.dot(xc.astype(BF16), w_ref[...], preferred_element_type=F32) + bias_ref[...]
    cblk = xc.shape[1]
    rt = _sigmoid(gates[:, :cblk])
    it = _sigmoid(gates[:, cblk:])
    neg_lam = -lam_ref[...]
    softplus = jnp.maximum(neg_lam, 0.0) + jnp.log1p(jnp.exp(-jnp.abs(neg_lam)))
    log_a = (-LRU_C) * rt * softplus
    a = jnp.exp(log_a)
    a_ref[...] = a
    b_ref[...] = jnp.sqrt(-jnp.tanh(log_a) * (a * a + 1.0)) * (it * xc)

    row = lax.broadcasted_iota(jnp.int32, (SUBLANES, cblk), 0)

    def group(gi, h_in):
        r0 = pl.multiple_of(gi * SUBLANES, SUBLANES)
        a = a_ref[pl.ds(r0, SUBLANES), :]
        b = b_ref[pl.ds(r0, SUBLANES), :]
        for d in (1, 2, 4):
            keep = row >= d
            a_sh = jnp.where(keep, pltpu.roll(a, d, axis=0), 1.0)
            b_sh = jnp.where(keep, pltpu.roll(b, d, axis=0), 0.0)
            b = a * b_sh + b
            a = a * a_sh
        hs = b + a * h_in
        o_ref[pl.ds(r0, SUBLANES), :] = (hs * y_ref[pl.ds(r0, SUBLANES), :]).astype(o_ref.dtype)
        return jnp.broadcast_to(hs[SUBLANES - 1:SUBLANES, :], (SUBLANES, cblk))

    h_ref[...] = lax.fori_loop(0, ts // SUBLANES, group, h_ref[...], unroll=4)


def _lru(xr, yr, conv_w, conv_b, wa, ba, wx, bx, lam, bsz, seq, ts=1024):
    r = xr.shape[-1]
    nblk = LRU_BLOCKS
    cblk = r // nblk
    w_cat = jnp.concatenate([wa, wx], axis=-1).astype(BF16)
    b_cat = jnp.concatenate([ba.reshape(nblk, 1, cblk), bx.reshape(nblk, 1, cblk)], axis=-1).astype(F32)
    kern = functools.partial(_lru_kernel, ts=ts)
    chan = lambda b, n, t: (0, n)
    return pl.pallas_call(
        kern,
        out_shape=jax.ShapeDtypeStruct((bsz, seq, r), BF16),
        grid=(bsz, nblk, seq // ts),
        in_specs=[pl.BlockSpec((None, ts, cblk), lambda b, n, t: (b, t, n)),
                  pl.BlockSpec((None, ts, cblk), lambda b, n, t: (b, t, n)),
                  pl.BlockSpec((CONV_WIDTH, cblk), chan),
                  pl.BlockSpec((1, cblk), chan),
                  pl.BlockSpec((None, cblk, 2 * cblk), lambda b, n, t: (n, 0, 0)),
                  pl.BlockSpec((None, 1, 2 * cblk), lambda b, n, t: (n, 0, 0)),
                  pl.BlockSpec((1, cblk), chan)],
        out_specs=pl.BlockSpec((None, ts, cblk), lambda b, n, t: (b, t, n)),
        scratch_shapes=[pltpu.VMEM((ts + SUBLANES, cblk), F32),
                        pltpu.VMEM((ts, cblk), F32),
                        pltpu.VMEM((ts, cblk), F32),
                        pltpu.VMEM((SUBLANES, cblk), F32)],
        compiler_params=_cparams(("parallel", "parallel", "arbitrary")),
        name="conv_rglru",
    )(xr, yr, conv_w.astype(F32), conv_b.reshape(1, r).astype(F32), w_cat, b_cat,
      lam.reshape(1, r).astype(F32))


def _merge_kernel(o_ref, hy_ref, wa_ref, wr_ref, ga_ref, gr_ref, out_ref):
    ya = jnp.dot(o_ref[...], wa_ref[...], preferred_element_type=F32)
    yr = jnp.dot(hy_ref[...], wr_ref[...], preferred_element_type=F32)
    out_ref[...] = (ga_ref[...] * ya + gr_ref[...] * yr).astype(out_ref.dtype)


def _merge(o, hy, wa, wr, gates, tm=1024, tn=512):
    t, k = o.shape
    kr = hy.shape[1]
    n = wa.shape[1]
    nj = n // tn
    return pl.pallas_call(
        _merge_kernel,
        out_shape=jax.ShapeDtypeStruct((t, n), BF16),
        grid=(t // tm, nj),
        in_specs=[pl.BlockSpec((tm, k), lambda i, j: (i, 0)),
                  pl.BlockSpec((tm, kr), lambda i, j: (i, 0)),
                  pl.BlockSpec((k, tn), lambda i, j: (0, j)),
                  pl.BlockSpec((kr, tn), lambda i, j: (0, j)),
                  pl.BlockSpec((tm, tn), lambda i, j: (i, j)),
                  pl.BlockSpec((tm, tn), lambda i, j: (i, nj + j))],
        out_specs=pl.BlockSpec((tm, tn), lambda i, j: (i, j)),
        compiler_params=_cparams(("parallel", "parallel")),
        name="branch_merge",
    )(o, hy, wa, wr, gates, gates)


def _outproj_kernel(m_ref, wo_ref, x_ref, g_ref, rw_ref, rb_ref, x1_ref, hf_ref, lg_ref):
    x1 = x_ref[...] + jnp.dot(m_ref[...], wo_ref[...], preferred_element_type=F32)
    x1_ref[...] = x1
    ms = jnp.mean(x1 * x1, axis=-1, keepdims=True)
    hf = (x1 * lax.rsqrt(ms + RMS_EPS) * g_ref[...]).astype(BF16)
    hf_ref[...] = hf
    lg_ref[...] = jnp.dot(hf, rw_ref[...], preferred_element_type=F32) + rb_ref[...]


def _outproj(merged, wo, x, g, rw, rb, tm=512):
    t, d = x.shape
    ne = rw.shape[1]
    rw_p = jnp.zeros((d, LANES), BF16).at[:, :ne].set(rw.astype(BF16))
    rb_p = jnp.zeros((1, LANES), F32).at[0, :ne].set(rb.astype(F32))
    row = lambda i: (i, 0)
    fixed = lambda i: (0, 0)
    return pl.pallas_call(
        _outproj_kernel,
        out_shape=(jax.ShapeDtypeStruct((t, d), F32),
                   jax.ShapeDtypeStruct((t, d), BF16),
                   jax.ShapeDtypeStruct((t, LANES), F32)),
        grid=(t // tm,),
        in_specs=[pl.BlockSpec((tm, d), row),
                  pl.BlockSpec((d, d), fixed),
                  pl.BlockSpec((tm, d), row),
                  pl.BlockSpec((1, d), fixed),
                  pl.BlockSpec((d, LANES), fixed),
                  pl.BlockSpec((1, LANES), fixed)],
        out_specs=(pl.BlockSpec((tm, d), row),
                   pl.BlockSpec((tm, d), row),
                   pl.BlockSpec((tm, LANES), row)),
        compiler_params=_cparams(("parallel",)),
        name="outproj_norm_router",
    )(merged, wo, x, g.reshape(1, d).astype(F32), rw_p, rb_p)


def _expert_kernel(be_ref, nu_ref, xs_ref, wg_ref, bg_ref, wu_ref, bu_ref, wd_ref, bd_ref,
                   y_ref, acc_ref):
    r = pl.program_id(0)
    f = pl.program_id(1)
    nf = pl.num_programs(1)
    used = r < nu_ref[0]

    @pl.when(jnp.logical_and(used, f == 0))
    def _():
        acc_ref[...] = jnp.zeros_like(acc_ref)

    @pl.when(used)
    def _():
        xs = xs_ref[...]
        glu = jnp.minimum(jnp.dot(xs, wg_ref[...], preferred_element_type=F32) + bg_ref[...],
                          SWIGLU_LIMIT)
        lin = jnp.clip(jnp.dot(xs, wu_ref[...], preferred_element_type=F32) + bu_ref[...],
                       -SWIGLU_LIMIT, SWIGLU_LIMIT)
        act = glu * _sigmoid(SWIGLU_ALPHA * glu) * (lin + 1.0)
        acc_ref[...] += jnp.dot(act.astype(BF16), wd_ref[...], preferred_element_type=F32)

    @pl.when(jnp.logical_and(used, f == nf - 1))
    def _():
        y_ref[...] = acc_ref[...] + bd_ref[...]

    @pl.when(jnp.logical_and(jnp.logical_not(used), f == nf - 1))
    def _():
        y_ref[...] = jnp.zeros_like(y_ref)


def _experts(xs, block_expert, n_used, wg, bg, wu, bu, wd, bd, tf=512):
    p, d = xs.shape
    ne, _, dff = wg.shape
    nblk = p // MOE_BLOCK
    nf = dff // tf

    def rr(r, nu):
        return jnp.minimum(r, nu[0] - 1)

    def ff(r, f, nu):
        return jnp.where(r < nu[0], f, nf - 1)

    grid_spec = pltpu.PrefetchScalarGridSpec(
        num_scalar_prefetch=2,
        grid=(nblk, nf),
        in_specs=[
            pl.BlockSpec((MOE_BLOCK, d), lambda r, f, be, nu: (rr(r, nu), 0)),
            pl.BlockSpec((None, d, tf), lambda r, f, be, nu: (be[rr(r, nu)], 0, ff(r, f, nu))),
            pl.BlockSpec((None, 1, tf), lambda r, f, be, nu: (be[rr(r, nu)], 0, ff(r, f, nu))),
            pl.BlockSpec((None, d, tf), lambda r, f, be, nu: (be[rr(r, nu)], 0, ff(r, f, nu))),
            pl.BlockSpec((None, 1, tf), lambda r, f, be, nu: (be[rr(r, nu)], 0, ff(r, f, nu))),
            pl.BlockSpec((None, tf, d), lambda r, f, be, nu: (be[rr(r, nu)], ff(r, f, nu), 0)),
            pl.BlockSpec((None, 1, d), lambda r, f, be, nu: (be[rr(r, nu)], 0, 0)),
        ],
        out_specs=pl.BlockSpec((MOE_BLOCK, d), lambda r, f, be, nu: (r, 0)),
        scratch_shapes=[pltpu.VMEM((MOE_BLOCK, d), F32)],
    )
    return pl.pallas_call(
        _expert_kernel,
        out_shape=jax.ShapeDtypeStruct((p, d), F32),
        grid_spec=grid_spec,
        compiler_params=_cparams(("arbitrary", "arbitrary")),
        name="moe_experts",
    )(block_expert, n_used, xs, wg, bg.reshape(ne, 1, dff).astype(F32), wu,
      bu.reshape(ne, 1, dff).astype(F32), wd, bd.reshape(ne, 1, d).astype(F32))


def _route(logits, n_experts):
    t = logits.shape[0]
    top_val, top_idx = lax.top_k(logits[:, :n_experts], MOE_TOPK)
    gates = jax.nn.softmax(top_val, axis=-1)
    n_asg = t * MOE_TOPK
    ids = top_idx.reshape(n_asg).astype(jnp.int32)
    blk = MOE_BLOCK
    p = (-(-n_asg // blk)) * blk + n_experts * blk
    counts = jnp.bincount(ids, length=n_experts).astype(jnp.int32)
    padded = (counts + blk - 1) // blk * blk
    pad_end = jnp.cumsum(padded)
    pad_start = pad_end - padded
    start = jnp.cumsum(counts) - counts
    order = jnp.argsort(ids)
    sorted_ids = ids[order]
    rank = jnp.arange(n_asg, dtype=jnp.int32) - start[sorted_ids]
    dest = jnp.zeros((n_asg,), jnp.int32).at[order].set(pad_start[sorted_ids] + rank)
    row_starts = jnp.arange(p // blk, dtype=jnp.int32) * blk
    block_expert = jnp.minimum(jnp.searchsorted(pad_end, row_starts, side='right'),
                               n_experts - 1).astype(jnp.int32)
    n_used = (pad_end[-1] // blk).astype(jnp.int32).reshape(1)
    row_tok = jnp.zeros((p,), jnp.int32).at[dest].set(jnp.arange(n_asg, dtype=jnp.int32) // MOE_TOPK)
    return gates, dest.reshape(t, MOE_TOPK), row_tok, block_expert, n_used


def _layer(x2, bsz, seq, norm_mix_g, w_in, conv_w, conv_b, lru_wa, lru_ba, lru_wx, lru_bx,
           lru_lambda, w_attn_branch, w_rnn_branch, w_out, norm_ffn_g, router_w, router_b,
           w_gate, b_gate, w_up, b_up, w_down, b_down):
    t, d = x2.shape
    aw = ATTN_HEADS * HEAD_DIM
    r = lru_lambda.shape[0]
    ne = router_w.shape[1]

    hm = _rmsnorm(x2, norm_mix_g, BF16)
    w_in_b = w_in.astype(BF16)
    qkv = _proj(hm, w_in_b, 0, 3 * aw, BF16, "scale_head",
                n_scaled_tiles=aw // 1024, scale=HEAD_DIM ** -0.5)
    xr = _proj(hm, w_in_b, 3 * aw, r, F32)
    yr = _proj(hm, w_in_b, 3 * aw + r, r, F32, "gelu")
    gates = _proj(hm, w_in_b, 3 * aw + 2 * r, 2 * d, F32, "sigmoid")

    o = _moba(qkv.reshape(bsz, seq, 3 * aw), bsz, seq).reshape(t, aw)
    hy = _lru(xr.reshape(bsz, seq, r), yr.reshape(bsz, seq, r), conv_w, conv_b,
              lru_wa, lru_ba, lru_wx, lru_bx, lru_lambda, bsz, seq).reshape(t, r)

    merged = _merge(o, hy, w_attn_branch.astype(BF16), w_rnn_branch.astype(BF16), gates)
    x1, hf, logits = _outproj(merged, w_out.astype(BF16), x2, norm_ffn_g, router_w, router_b)

    gate_w, dest, row_tok, block_expert, n_used = _route(logits, ne)
    xs = hf[row_tok]
    y = _experts(xs, block_expert, n_used, w_gate.astype(BF16), b_gate, w_up.astype(BF16), b_up,
                 w_down.astype(BF16), b_down)
    moe_out = jnp.sum(y[dest] * gate_w[:, :, None], axis=1)
    return x1, moe_out


def kernel(x, norm_mix_g, w_in, conv_w, conv_b, lru_wa, lru_ba, lru_wx, lru_bx, lru_lambda, w_attn_branch, w_rnn_branch, w_out, norm_ffn_g, router_w, router_b, w_gate, b_gate, w_up, b_up, w_down, b_down, norm_final_g):
    bsz, seq, d = x.shape
    depth = w_in.shape[0]
    assert seq % MOBA_BLOCK == 0 and d % LANES == 0
    x2 = x.reshape(bsz * seq, d)
    out = None
    for l in range(depth):
        x1, moe_out = _layer(
            x2, bsz, seq, norm_mix_g[l], w_in[l], conv_w[l], conv_b[l], lru_wa[l], lru_ba[l],
            lru_wx[l], lru_bx[l], lru_lambda[l], w_attn_branch[l], w_rnn_branch[l], w_out[l],
            norm_ffn_g[l], router_w[l], router_b[l], w_gate[l], b_gate[l], w_up[l], b_up[l],
            w_down[l], b_down[l])
        if l + 1 < depth:
            x2 = x1 + moe_out
        else:
            out = _add_rmsnorm(x1, moe_out, norm_final_g)
    return out.reshape(bsz, seq, d)
```

```python
import functools

import jax
import jax.numpy as jnp
from jax import lax
from jax.experimental import pallas as pl
from jax.experimental.pallas import tpu as pltpu

ATTN_HEADS = 16
HEAD_DIM = 128
MOBA_BLOCK = 256
MOBA_TOPK = 3
LRU_BLOCKS = 8
CONV_WIDTH = 4
LRU_C = 8.0
MOE_TOPK = 4
SWIGLU_ALPHA = 1.702
SWIGLU_LIMIT = 7.0
MOE_BLOCK = 512
RMS_EPS = 1e-6

LANES = 128
SUBLANES = 8
VMEM_LIMIT = 56 * 1024 * 1024
LOG2E = 1.4426950408889634
NEG_BIG = -1e30

BF16 = jnp.bfloat16
F32 = jnp.float32


def _cparams(sem):
    return pltpu.CompilerParams(dimension_semantics=sem, vmem_limit_bytes=VMEM_LIMIT)


def _sigmoid(x):
    return 1.0 / (1.0 + jnp.exp(-x))


def _gelu_tanh(x):
    c = 0.7978845608028654
    return 0.5 * x * (1.0 + jnp.tanh(c * (x + 0.044715 * (x * x * x))))


def _rmsnorm_kernel(x_ref, g_ref, o_ref):
    x = x_ref[...]
    ms = jnp.mean(x * x, axis=-1, keepdims=True)
    o_ref[...] = (x * lax.rsqrt(ms + RMS_EPS) * g_ref[...]).astype(o_ref.dtype)


def _rmsnorm(x, g, out_dtype, tm=1024):
    t, d = x.shape
    return pl.pallas_call(
        _rmsnorm_kernel,
        out_shape=jax.ShapeDtypeStruct((t, d), out_dtype),
        grid=(t // tm,),
        in_specs=[pl.BlockSpec((tm, d), lambda i: (i, 0)),
                  pl.BlockSpec((1, d), lambda i: (0, 0))],
        out_specs=pl.BlockSpec((tm, d), lambda i: (i, 0)),
        compiler_params=_cparams(("parallel",)),
        name="rmsnorm",
    )(x, g.reshape(1, d).astype(F32))


def _proj_kernel(a_ref, w_ref, o_ref, *, epilogue, n_scaled_tiles, scale):
    acc = jnp.dot(a_ref[...], w_ref[...], preferred_element_type=F32)
    if epilogue == "scale_head":
        s = jnp.where(pl.program_id(1) < n_scaled_tiles, scale, 1.0).astype(F32)
        acc = acc * s
    elif epilogue == "gelu":
        acc = _gelu_tanh(acc)
    elif epilogue == "sigmoid":
        acc = _sigmoid(acc)
    o_ref[...] = acc.astype(o_ref.dtype)


def _proj(a, w, col0, n, out_dtype, epilogue="none", n_scaled_tiles=0, scale=1.0,
          tm=1024, tn=1024):
    t, k = a.shape
    c0 = col0 // tn
    kern = functools.partial(_proj_kernel, epilogue=epilogue,
                             n_scaled_tiles=n_scaled_tiles, scale=scale)
    return pl.pallas_call(
        kern,
        out_shape=jax.ShapeDtypeStruct((t, n), out_dtype),
        grid=(t // tm, n // tn),
        in_specs=[pl.BlockSpec((tm, k), lambda i, j: (i, 0)),
                  pl.BlockSpec((k, tn), lambda i, j: (0, c0 + j))],
        out_specs=pl.BlockSpec((tm, tn), lambda i, j: (i, j)),
        compiler_params=_cparams(("parallel", "parallel")),
        name="proj_" + epilogue,
    )(a, w)


def _dot_nt(a, b):
    return lax.dot_general(a, b, (((1,), (1,)), ((), ())), preferred_element_type=F32)


def _moba_kernel(q_ref, k_ref, v_ref, o_ref, kext_ref, vt_ref, *, seq):
    blk = MOBA_BLOCK
    qt = 2 * blk
    nsb = seq // qt

    kext_ref[:, :HEAD_DIM] = k_ref[...]
    r_blk = lax.broadcasted_iota(jnp.int32, (seq, LANES), 0) // blk
    c_id = lax.broadcasted_iota(jnp.int32, (seq, LANES), 1)
    kext_ref[:, HEAD_DIM:] = jnp.where(r_blk == c_id, 1.0, 0.0).astype(BF16)
    vt_ref[...] = v_ref[...].T

    a_row = lax.broadcasted_iota(jnp.int32, (LANES, seq), 0)
    a_col = lax.broadcasted_iota(jnp.int32, (LANES, seq), 1) // blk
    avg = jnp.where(a_row == a_col, 1.0 / blk, 0.0).astype(BF16)
    kmean = jnp.dot(avg, k_ref[...], preferred_element_type=F32).astype(BF16)

    nrow = 32
    brow = lax.broadcasted_iota(jnp.int32, (nrow, qt), 0)
    second = lax.broadcasted_iota(jnp.int32, (nrow, qt), 1) >= blk
    key_i = lax.broadcasted_iota(jnp.int32, (qt, qt), 0)
    qry_i = lax.broadcasted_iota(jnp.int32, (qt, qt), 1)
    same_blk = (key_i >= blk) == (qry_i >= blk)
    causal_self = jnp.logical_and(same_blk, key_i <= qry_i)
    cross = jnp.logical_and(key_i < blk, qry_i >= blk)

    def q_super(a, _):
        row0 = pl.multiple_of(a * qt, qt)
        q = q_ref[pl.ds(row0, qt), :]
        gate = _dot_nt(kmean, q)[:nrow, :]
        qblk = 2 * a + second.astype(jnp.int32)
        past = brow < qblk
        g = jnp.where(past, gate, -jnp.inf)
        sel = jnp.zeros((nrow, qt), jnp.bool_)
        for _k in range(MOBA_TOPK):
            mx = jnp.max(g, axis=0, keepdims=True)
            first = jnp.min(jnp.where(g == mx, brow, nrow), axis=0, keepdims=True)
            pick = brow == first
            sel = jnp.logical_or(sel, pick)
            g = jnp.where(pick, -jnp.inf, g)
        chosen = jnp.logical_and(sel, past)
        bias_t = jnp.where(chosen, 0.0, NEG_BIG)
        bias_t = jnp.concatenate([bias_t, jnp.zeros((LANES - nrow, qt), F32)], axis=0)
        qext = jnp.concatenate([q, bias_t.T.astype(BF16)], axis=1)

        prev_sel = jnp.max(jnp.where(jnp.logical_and(chosen, brow == 2 * a), 1.0, 0.0),
                           axis=0, keepdims=True) > 0.5
        allowed = jnp.logical_or(causal_self, jnp.logical_and(cross, prev_sel))
        s = _dot_nt(k_ref[pl.ds(row0, qt), :], q)
        s = jnp.where(allowed, s, NEG_BIG)
        m = jnp.max(s, axis=0, keepdims=True)
        p = jnp.exp2(s - m)
        l = jnp.sum(p, axis=0, keepdims=True)
        acc = jnp.dot(vt_ref[:, pl.ds(row0, qt)], p.astype(BF16), preferred_element_type=F32)

        def kv_super(b, carry):
            m, l, acc = carry
            c0 = pl.multiple_of(b * qt, qt)
            s = _dot_nt(kext_ref[pl.ds(c0, qt), :], qext)
            m_new = jnp.maximum(m, jnp.max(s, axis=0, keepdims=True))
            alpha = jnp.exp2(m - m_new)
            p = jnp.exp2(s - m_new)
            l = alpha * l + jnp.sum(p, axis=0, keepdims=True)
            acc = alpha * acc + jnp.dot(vt_ref[:, pl.ds(c0, qt)], p.astype(BF16),
                                        preferred_element_type=F32)
            return m_new, l, acc

        m, l, acc = lax.fori_loop(0, a, kv_super, (m, l, acc))
        o_ref[pl.ds(row0, qt), :] = (acc * (1.0 / l)).T.astype(o_ref.dtype)
        return 0

    lax.fori_loop(0, nsb, q_super, 0)


def _moba(qkv, bsz, seq):
    h = ATTN_HEADS
    kern = functools.partial(_moba_kernel, seq=seq)
    return pl.pallas_call(
        kern,
        out_shape=jax.ShapeDtypeStruct((bsz, seq, h * HEAD_DIM), BF16),
        grid=(bsz, h),
        in_specs=[pl.BlockSpec((None, seq, HEAD_DIM), lambda b, hh: (b, 0, hh)),
                  pl.BlockSpec((None, seq, HEAD_DIM), lambda b, hh: (b, 0, h + hh)),
                  pl.BlockSpec((None, seq, HEAD_DIM), lambda b, hh: (b, 0, 2 * h + hh))],
        out_specs=pl.BlockSpec((None, seq, HEAD_DIM), lambda b, hh: (b, 0, hh)),
        scratch_shapes=[pltpu.VMEM((seq, 2 * HEAD_DIM), BF16),
                        pltpu.VMEM((HEAD_DIM, seq), BF16)],
        compiler_params=_cparams(("parallel", "parallel")),
        name="moba_attention",
    )(qkv, qkv, qkv)


def _lru_kernel(x_ref, y_ref, cw_ref, cb_ref, w_ref, bias_ref, lam_ref, o_ref,
                win_ref, a_ref, b_ref, h_ref, *, ts):
    t_idx = pl.program_id(2)

    @pl.when(t_idx == 0)
    def _():
        win_ref[0:SUBLANES, :] = jnp.zeros((SUBLANES, win_ref.shape[1]), F32)
        h_ref[...] = jnp.zeros_like(h_ref)

    win_ref[SUBLANES:, :] = x_ref[...]
    xc = cb_ref[...] + cw_ref[CONV_WIDTH - 1:CONV_WIDTH, :] * x_ref[...]
    for k in range(CONV_WIDTH - 1):
        shift = CONV_WIDTH - 1 - k
        xc = xc + cw_ref[k:k + 1, :] * win_ref[SUBLANES - shift:SUBLANES - shift + ts, :]
    win_ref[0:SUBLANES, :] = win_ref[ts:ts + SUBLANES, :]

    gates = jnp.dot(xc.astype(BF16), w_ref[...], preferred_element_type=F32) + bias_ref[...]
    cblk = xc.shape[1]
    rt = _sigmoid(gates[:, :cblk])
    it = _sigmoid(gates[:, cblk:])
    neg_lam = -lam_ref[...]
    softplus = jnp.maximum(neg_lam, 0.0) + jnp.log1p(jnp.exp(-jnp.abs(neg_lam)))
    log_a = (-LRU_C) * rt * softplus
    a = jnp.exp(log_a)
    a_ref[...] = a
    b_ref[...] = jnp.sqrt(-jnp.tanh(log_a) * (a * a + 1.0)) * (it * xc)

    row = lax.broadcasted_iota(jnp.int32, (SUBLANES, cblk), 0)

    def group(gi, h_in):
        r0 = pl.multiple_of(gi * SUBLANES, SUBLANES)
        a = a_ref[pl.ds(r0, SUBLANES), :]
        b = b_ref[pl.ds(r0, SUBLANES), :]
        for d in (1, 2, 4):
            keep = row >= d
            a_sh = jnp.where(keep, pltpu.roll(a, d, axis=0), 1.0)
            b_sh = jnp.where(keep, pltpu.roll(b, d, axis=0), 0.0)
            b = a * b_sh + b
            a = a * a_sh
        hs = b + a * h_in
        o_ref[pl.ds(r0, SUBLANES), :] = (hs * y_ref[pl.ds(r0, SUBLANES), :]).astype(o_ref.dtype)
        return jnp.broadcast_to(hs[SUBLANES - 1:SUBLANES, :], (SUBLANES, cblk))

    h_ref[...] = lax.fori_loop(0, ts // SUBLANES, group, h_ref[...], unroll=4)


def _lru(xr, yr, conv_w, conv_b, wa, ba, wx, bx, lam, bsz, seq, ts=1024):
    r = xr.shape[-1]
    nblk = LRU_BLOCKS
    cblk = r // nblk
    w_cat = jnp.concatenate([wa, wx], axis=-1).astype(BF16)
    b_cat = jnp.concatenate([ba.reshape(nblk, 1, cblk), bx.reshape(nblk, 1, cblk)], axis=-1).astype(F32)
    kern = functools.partial(_lru_kernel, ts=ts)
    chan = lambda b, n, t: (0, n)
    return pl.pallas_call(
        kern,
        out_shape=jax.ShapeDtypeStruct((bsz, seq, r), BF16),
        grid=(bsz, nblk, seq // ts),
        in_specs=[pl.BlockSpec((None, ts, cblk), lambda b, n, t: (b, t, n)),
                  pl.BlockSpec((None, ts, cblk), lambda b, n, t: (b, t, n)),
                  pl.BlockSpec((CONV_WIDTH, cblk), chan),
                  pl.BlockSpec((1, cblk), chan),
                  pl.BlockSpec((None, cblk, 2 * cblk), lambda b, n, t: (n, 0, 0)),
                  pl.BlockSpec((None, 1, 2 * cblk), lambda b, n, t: (n, 0, 0)),
                  pl.BlockSpec((1, cblk), chan)],
        out_specs=pl.BlockSpec((None, ts, cblk), lambda b, n, t: (b, t, n)),
        scratch_shapes=[pltpu.VMEM((ts + SUBLANES, cblk), F32),
                        pltpu.VMEM((ts, cblk), F32),
                        pltpu.VMEM((ts, cblk), F32),
                        pltpu.VMEM((SUBLANES, cblk), F32)],
        compiler_params=_cparams(("parallel", "parallel", "arbitrary")),
        name="conv_rglru",
    )(xr, yr, conv_w.astype(F32), conv_b.reshape(1, r).astype(F32), w_cat, b_cat,
      lam.reshape(1, r).astype(F32))


def _merge_kernel(o_ref, hy_ref, wa_ref, wr_ref, ga_ref, gr_ref, out_ref):
    ya = jnp.dot(o_ref[...], wa_ref[...], preferred_element_type=F32)
    yr = jnp.dot(hy_ref[...], wr_ref[...], preferred_element_type=F32)
    out_ref[...] = (ga_ref[...] * ya + gr_ref[...] * yr).astype(out_ref.dtype)


def _merge(o, hy, wa, wr, gates, tm=1024, tn=512):
    t, k = o.shape
    kr = hy.shape[1]
    n = wa.shape[1]
    nj = n // tn
    return pl.pallas_call(
        _merge_kernel,
        out_shape=jax.ShapeDtypeStruct((t, n), BF16),
        grid=(t // tm, nj),
        in_specs=[pl.BlockSpec((tm, k), lambda i, j: (i, 0)),
                  pl.BlockSpec((tm, kr), lambda i, j: (i, 0)),
                  pl.BlockSpec((k, tn), lambda i, j: (0, j)),
                  pl.BlockSpec((kr, tn), lambda i, j: (0, j)),
                  pl.BlockSpec((tm, tn), lambda i, j: (i, j)),
                  pl.BlockSpec((tm, tn), lambda i, j: (i, nj + j))],
        out_specs=pl.BlockSpec((tm, tn), lambda i, j: (i, j)),
        compiler_params=_cparams(("parallel", "parallel")),
        name="branch_merge",
    )(o, hy, wa, wr, gates, gates)


def _outproj_kernel(m_ref, wo_ref, x_ref, g_ref, rw_ref, rb_ref, x1_ref, hf_ref, lg_ref):
    x1 = x_ref[...] + jnp.dot(m_ref[...], wo_ref[...], preferred_element_type=F32)
    x1_ref[...] = x1
    ms = jnp.mean(x1 * x1, axis=-1, keepdims=True)
    hf = (x1 * lax.rsqrt(ms + RMS_EPS) * g_ref[...]).astype(BF16)
    hf_ref[...] = hf
    lg_ref[...] = jnp.dot(hf, rw_ref[...], preferred_element_type=F32) + rb_ref[...]


def _outproj(merged, wo, x, g, rw, rb, tm=512):
    t, d = x.shape
    ne = rw.shape[1]
    rw_p = jnp.zeros((d, LANES), BF16).at[:, :ne].set(rw.astype(BF16))
    rb_p = jnp.zeros((1, LANES), F32).at[0, :ne].set(rb.astype(F32))
    row = lambda i: (i, 0)
    fixed = lambda i: (0, 0)
    return pl.pallas_call(
        _outproj_kernel,
        out_shape=(jax.ShapeDtypeStruct((t, d), F32),
                   jax.ShapeDtypeStruct((t, d), BF16),
                   jax.ShapeDtypeStruct((t, LANES), F32)),
        grid=(t // tm,),
        in_specs=[pl.BlockSpec((tm, d), row),
                  pl.BlockSpec((d, d), fixed),
                  pl.BlockSpec((tm, d), row),
                  pl.BlockSpec((1, d), fixed),
                  pl.BlockSpec((d, LANES), fixed),
                  pl.BlockSpec((1, LANES), fixed)],
        out_specs=(pl.BlockSpec((tm, d), row),
                   pl.BlockSpec((tm, d), row),
                   pl.BlockSpec((tm, LANES), row)),
        compiler_params=_cparams(("parallel",)),
        name="outproj_norm_router",
    )(merged, wo, x, g.reshape(1, d).astype(F32), rw_p, rb_p)


def _expert_changed(be_ref, r):
    return jnp.logical_or(r == 0, be_ref[r] != be_ref[jnp.maximum(r - 1, 0)])


def _expert_up_kernel(be_ref, nu_ref, xs_ref, wg_ref, bg_ref, wu_ref, bu_ref, act_ref,
                      wgb_ref, wub_ref):
    r = pl.program_id(1)
    used = r < nu_ref[0]

    @pl.when(jnp.logical_and(used, _expert_changed(be_ref, r)))
    def _():
        wgb_ref[...] = wg_ref[...].astype(BF16)
        wub_ref[...] = wu_ref[...].astype(BF16)

    @pl.when(used)
    def _():
        xs = xs_ref[...]
        glu = jnp.minimum(jnp.dot(xs, wgb_ref[...], preferred_element_type=F32) + bg_ref[...],
                          SWIGLU_LIMIT)
        lin = jnp.clip(jnp.dot(xs, wub_ref[...], preferred_element_type=F32) + bu_ref[...],
                       -SWIGLU_LIMIT, SWIGLU_LIMIT)
        act_ref[...] = (glu * _sigmoid(SWIGLU_ALPHA * glu) * (lin + 1.0)).astype(act_ref.dtype)

    @pl.when(jnp.logical_not(used))
    def _():
        act_ref[...] = jnp.zeros_like(act_ref)


def _expert_down_kernel(be_ref, nu_ref, act_ref, wd_ref, bd_ref, y_ref, wdb_ref):
    r = pl.program_id(1)
    used = r < nu_ref[0]

    @pl.when(jnp.logical_and(used, _expert_changed(be_ref, r)))
    def _():
        wdb_ref[...] = wd_ref[...].astype(BF16)

    @pl.when(used)
    def _():
        y_ref[...] = jnp.dot(act_ref[...], wdb_ref[...], preferred_element_type=F32) + bd_ref[...]

    @pl.when(jnp.logical_not(used))
    def _():
        y_ref[...] = jnp.zeros_like(y_ref)


def _experts(xs, block_expert, n_used, wg, bg, wu, bu, wd, bd, tf=512, tn=512):
    p, d = xs.shape
    ne, _, dff = wg.shape
    nblk = p // MOE_BLOCK

    def rr(r, nu):
        return jnp.minimum(r, nu[0] - 1)

    wspec = lambda k, t: pl.BlockSpec((None, k, t), lambda c, r, be, nu: (be[rr(r, nu)], 0, c))
    bspec = lambda t: pl.BlockSpec((None, 1, t), lambda c, r, be, nu: (be[rr(r, nu)], 0, c))
    rows = lambda k: pl.BlockSpec((MOE_BLOCK, k), lambda c, r, be, nu: (rr(r, nu), 0))

    act = pl.pallas_call(
        _expert_up_kernel,
        out_shape=jax.ShapeDtypeStruct((p, dff), BF16),
        grid_spec=pltpu.PrefetchScalarGridSpec(
            num_scalar_prefetch=2,
            grid=(dff // tf, nblk),
            in_specs=[rows(d), wspec(d, tf), bspec(tf), wspec(d, tf), bspec(tf)],
            out_specs=pl.BlockSpec((MOE_BLOCK, tf), lambda c, r, be, nu: (r, c)),
            scratch_shapes=[pltpu.VMEM((d, tf), BF16), pltpu.VMEM((d, tf), BF16)]),
        compiler_params=_cparams(("arbitrary", "arbitrary")),
        name="moe_up",
    )(block_expert, n_used, xs, wg, bg.reshape(ne, 1, dff).astype(F32), wu,
      bu.reshape(ne, 1, dff).astype(F32))

    return pl.pallas_call(
        _expert_down_kernel,
        out_shape=jax.ShapeDtypeStruct((p, d), F32),
        grid_spec=pltpu.PrefetchScalarGridSpec(
            num_scalar_prefetch=2,
            grid=(d // tn, nblk),
            in_specs=[rows(dff), wspec(dff, tn), bspec(tn)],
            out_specs=pl.BlockSpec((MOE_BLOCK, tn), lambda c, r, be, nu: (r, c)),
            scratch_shapes=[pltpu.VMEM((dff, tn), BF16)]),
        compiler_params=_cparams(("arbitrary", "arbitrary")),
        name="moe_down",
    )(block_expert, n_used, act, wd, bd.reshape(ne, 1, d).astype(F32))


def _combine_kernel(dest_ref, x1_ref, gw_ref, g_ref, y_hbm, o_ref, buf_ref, sem_ref,
                    *, tt, ct, norm):
    base = pl.program_id(0) * (tt * MOE_TOPK)
    nch = tt // ct

    def row_copy(row, slot, k, tok):
        return pltpu.make_async_copy(y_hbm.at[pl.ds(row, 1), :],
                                     buf_ref.at[slot, k, pl.ds(tok, 1), :], sem_ref.at[slot])

    def issue(c):
        def body(tok, _):
            for k in range(MOE_TOPK):
                row = dest_ref[base + (c * ct + tok) * MOE_TOPK + k]
                row_copy(row, c % 2, k, tok).start()
            return 0
        lax.fori_loop(0, ct, body, 0, unroll=2)

    def wait(c):
        for k in range(MOE_TOPK):
            pltpu.make_async_copy(y_hbm.at[pl.ds(0, ct), :], buf_ref.at[c % 2, k],
                                  sem_ref.at[c % 2]).wait()

    issue(0)
    for c in range(nch):
        if c + 1 < nch:
            issue(c + 1)
        wait(c)
        rows = pl.ds(c * ct, ct)
        acc = x1_ref[rows, :]
        gw = gw_ref[rows, :]
        for k in range(MOE_TOPK):
            acc = acc + gw[:, k:k + 1] * buf_ref[c % 2, k]
        if norm:
            ms = jnp.mean(acc * acc, axis=-1, keepdims=True)
            acc = acc * lax.rsqrt(ms + RMS_EPS) * g_ref[...]
        o_ref[rows, :] = acc


def _combine(x1, y, dest, gate_w, g, norm, tt=256, ct=64):
    t, d = x1.shape
    kern = functools.partial(_combine_kernel, tt=tt, ct=ct, norm=norm)
    return pl.pallas_call(
        kern,
        out_shape=jax.ShapeDtypeStruct((t, d), F32),
        grid_spec=pltpu.PrefetchScalarGridSpec(
            num_scalar_prefetch=1,
            grid=(t // tt,),
            in_specs=[pl.BlockSpec((tt, d), lambda i, dref: (i, 0)),
                      pl.BlockSpec((tt, MOE_TOPK), lambda i, dref: (i, 0)),
                      pl.BlockSpec((1, d), lambda i, dref: (0, 0)),
                      pl.BlockSpec(memory_space=pl.ANY)],
            out_specs=pl.BlockSpec((tt, d), lambda i, dref: (i, 0)),
            scratch_shapes=[pltpu.VMEM((2, MOE_TOPK, ct, d), F32),
                            pltpu.SemaphoreType.DMA((2,))]),
        compiler_params=_cparams(("arbitrary",)),
        name="moe_combine",
    )(dest.reshape(t * MOE_TOPK), x1, gate_w, g.reshape(1, d).astype(F32), y)


def _route(logits, n_experts):
    t = logits.shape[0]
    top_val, top_idx = lax.top_k(logits[:, :n_experts], MOE_TOPK)
    gates = jax.nn.softmax(top_val, axis=-1)
    n_asg = t * MOE_TOPK
    ids = top_idx.reshape(n_asg).astype(jnp.int32)
    blk = MOE_BLOCK
    p = (-(-n_asg // blk)) * blk + n_experts * blk
    counts = jnp.bincount(ids, length=n_experts).astype(jnp.int32)
    padded = (counts + blk - 1) // blk * blk
    pad_end = jnp.cumsum(padded)
    pad_start = pad_end - padded
    start = jnp.cumsum(counts) - counts
    order = jnp.argsort(ids)
    sorted_ids = ids[order]
    rank = jnp.arange(n_asg, dtype=jnp.int32) - start[sorted_ids]
    dest = jnp.zeros((n_asg,), jnp.int32).at[order].set(pad_start[sorted_ids] + rank)
    row_starts = jnp.arange(p // blk, dtype=jnp.int32) * blk
    block_expert = jnp.minimum(jnp.searchsorted(pad_end, row_starts, side='right'),
                               n_experts - 1).astype(jnp.int32)
    n_used = (pad_end[-1] // blk).astype(jnp.int32).reshape(1)
    row_tok = jnp.zeros((p,), jnp.int32).at[dest].set(jnp.arange(n_asg, dtype=jnp.int32) // MOE_TOPK)
    return gates, dest.reshape(t, MOE_TOPK), row_tok, block_expert, n_used


def _layer(x2, bsz, seq, norm_mix_g, w_in, conv_w, conv_b, lru_wa, lru_ba, lru_wx, lru_bx,
           lru_lambda, w_attn_branch, w_rnn_branch, w_out, norm_ffn_g, router_w, router_b,
           w_gate, b_gate, w_up, b_up, w_down, b_down):
    t, d = x2.shape
    aw = ATTN_HEADS * HEAD_DIM
    r = lru_lambda.shape[0]
    ne = router_w.shape[1]

    hm = _rmsnorm(x2, norm_mix_g, BF16)
    w_in_b = w_in.astype(BF16)
    qkv = _proj(hm, w_in_b, 0, 3 * aw, BF16, "scale_head",
                n_scaled_tiles=aw // 1024, scale=HEAD_DIM ** -0.5 * LOG2E)
    xr = _proj(hm, w_in_b, 3 * aw, r, F32)
    yr = _proj(hm, w_in_b, 3 * aw + r, r, F32, "gelu")
    gates = _proj(hm, w_in_b, 3 * aw + 2 * r, 2 * d, F32, "sigmoid")

    o = _moba(qkv.reshape(bsz, seq, 3 * aw), bsz, seq).reshape(t, aw)
    hy = _lru(xr.reshape(bsz, seq, r), yr.reshape(bsz, seq, r), conv_w, conv_b,
              lru_wa, lru_ba, lru_wx, lru_bx, lru_lambda, bsz, seq).reshape(t, r)

    merged = _merge(o, hy, w_attn_branch.astype(BF16), w_rnn_branch.astype(BF16), gates)
    x1, hf, logits = _outproj(merged, w_out.astype(BF16), x2, norm_ffn_g, router_w, router_b)

    gate_w, dest, row_tok, block_expert, n_used = _route(logits, ne)
    xs = hf[row_tok]
    y = _experts(xs, block_expert, n_used, w_gate, b_gate, w_up, b_up, w_down, b_down)
    return x1, y, dest, gate_w


def kernel(x, norm_mix_g, w_in, conv_w, conv_b, lru_wa, lru_ba, lru_wx, lru_bx, lru_lambda, w_attn_branch, w_rnn_branch, w_out, norm_ffn_g, router_w, router_b, w_gate, b_gate, w_up, b_up, w_down, b_down, norm_final_g):
    bsz, seq, d = x.shape
    depth = w_in.shape[0]
    assert seq % (2 * MOBA_BLOCK) == 0 and d % LANES == 0
    x2 = x.reshape(bsz * seq, d)
    for l in range(depth):
        x1, y, dest, gate_w = _layer(
            x2, bsz, seq, norm_mix_g[l], w_in[l], conv_w[l], conv_b[l], lru_wa[l], lru_ba[l],
            lru_wx[l], lru_bx[l], lru_lambda[l], w_attn_branch[l], w_rnn_branch[l], w_out[l],
            norm_ffn_g[l], router_w[l], router_b[l], w_gate[l], b_gate[l], w_up[l], b_up[l],
            w_down[l], b_down[l])
        last = l + 1 == depth
        x2 = _combine(x1, y, dest, gate_w, norm_final_g, norm=last)
    return x2.reshape(bsz, seq, d)
```

```python
import functools

import jax
import jax.numpy as jnp
from jax import lax
from jax.experimental import pallas as pl
from jax.experimental.pallas import tpu as pltpu

ATTN_HEADS = 16
HEAD_DIM = 128
MOBA_BLOCK = 256
MOBA_TOPK = 3
LRU_BLOCKS = 8
CONV_WIDTH = 4
LRU_C = 8.0
MOE_TOPK = 4
SWIGLU_ALPHA = 1.702
SWIGLU_LIMIT = 7.0
MOE_BLOCK = 512
RMS_EPS = 1e-6

LANES = 128
SUBLANES = 8
VMEM_LIMIT = 56 * 1024 * 1024
LOG2E = 1.4426950408889634
NEG_BIG = -1e30

BF16 = jnp.bfloat16
F32 = jnp.float32


def _cparams(sem):
    return pltpu.CompilerParams(dimension_semantics=sem, vmem_limit_bytes=VMEM_LIMIT)


def _sigmoid(x):
    return 1.0 / (1.0 + jnp.exp(-x))


def _gelu_tanh(x):
    c = 0.7978845608028654
    return 0.5 * x * (1.0 + jnp.tanh(c * (x + 0.044715 * (x * x * x))))


def _rmsnorm_kernel(x_ref, g_ref, o_ref):
    x = x_ref[...]
    ms = jnp.mean(x * x, axis=-1, keepdims=True)
    o_ref[...] = (x * lax.rsqrt(ms + RMS_EPS) * g_ref[...]).astype(o_ref.dtype)


def _rmsnorm(x, g, out_dtype, tm=1024):
    t, d = x.shape
    return pl.pallas_call(
        _rmsnorm_kernel,
        out_shape=jax.ShapeDtypeStruct((t, d), out_dtype),
        grid=(t // tm,),
        in_specs=[pl.BlockSpec((tm, d), lambda i: (i, 0)),
                  pl.BlockSpec((1, d), lambda i: (0, 0))],
        out_specs=pl.BlockSpec((tm, d), lambda i: (i, 0)),
        compiler_params=_cparams(("parallel",)),
        name="rmsnorm",
    )(x, g.reshape(1, d).astype(F32))


def _proj_kernel(a_ref, w_ref, o_ref, *, epilogue, n_scaled_tiles, scale):
    acc = jnp.dot(a_ref[...], w_ref[...], preferred_element_type=F32)
    if epilogue == "scale_head":
        s = jnp.where(pl.program_id(1) < n_scaled_tiles, scale, 1.0).astype(F32)
        acc = acc * s
    elif epilogue == "gelu":
        acc = _gelu_tanh(acc)
    elif epilogue == "sigmoid":
        acc = _sigmoid(acc)
    o_ref[...] = acc.astype(o_ref.dtype)


def _proj(a, w, col0, n, out_dtype, epilogue="none", n_scaled_tiles=0, scale=1.0,
          tm=1024, tn=1024):
    t, k = a.shape
    c0 = col0 // tn
    kern = functools.partial(_proj_kernel, epilogue=epilogue,
                             n_scaled_tiles=n_scaled_tiles, scale=scale)
    return pl.pallas_call(
        kern,
        out_shape=jax.ShapeDtypeStruct((t, n), out_dtype),
        grid=(t // tm, n // tn),
        in_specs=[pl.BlockSpec((tm, k), lambda i, j: (i, 0)),
                  pl.BlockSpec((k, tn), lambda i, j: (0, c0 + j))],
        out_specs=pl.BlockSpec((tm, tn), lambda i, j: (i, j)),
        compiler_params=_cparams(("parallel", "parallel")),
        name="proj_" + epilogue,
    )(a, w)


def _dot_nt(a, b):
    return lax.dot_general(a, b, (((1,), (1,)), ((), ())), preferred_element_type=F32)


def _moba_kernel(q_ref, k_ref, v_ref, o_ref, kext_ref, vt_ref, s_ref, *, seq):
    blk = MOBA_BLOCK
    qt = 2 * blk
    nsb = seq // qt

    kext_ref[:, :HEAD_DIM] = k_ref[...]
    r_blk = lax.broadcasted_iota(jnp.int32, (seq, LANES), 0) // blk
    c_id = lax.broadcasted_iota(jnp.int32, (seq, LANES), 1)
    kext_ref[:, HEAD_DIM:] = jnp.where(r_blk == c_id, 1.0, 0.0).astype(BF16)
    vt_ref[...] = v_ref[...].T

    a_row = lax.broadcasted_iota(jnp.int32, (LANES, seq), 0)
    a_col = lax.broadcasted_iota(jnp.int32, (LANES, seq), 1) // blk
    avg = jnp.where(a_row == a_col, 1.0 / blk, 0.0).astype(BF16)
    kmean = jnp.dot(avg, k_ref[...], preferred_element_type=F32).astype(BF16)

    nrow = 32
    brow = lax.broadcasted_iota(jnp.int32, (nrow, qt), 0)
    second = lax.broadcasted_iota(jnp.int32, (nrow, qt), 1) >= blk
    key_i = lax.broadcasted_iota(jnp.int32, (qt, qt), 0)
    qry_i = lax.broadcasted_iota(jnp.int32, (qt, qt), 1)
    same_blk = (key_i >= blk) == (qry_i >= blk)
    causal_self = jnp.logical_and(same_blk, key_i <= qry_i)
    cross = jnp.logical_and(key_i < blk, qry_i >= blk)

    def q_super(a, _):
        row0 = pl.multiple_of(a * qt, qt)
        q = q_ref[pl.ds(row0, qt), :]
        gate = _dot_nt(kmean, q)[:nrow, :]
        qblk = 2 * a + second.astype(jnp.int32)
        past = brow < qblk
        g = jnp.where(past, gate, -jnp.inf)
        sel = jnp.zeros((nrow, qt), jnp.bool_)
        for _k in range(MOBA_TOPK):
            mx = jnp.max(g, axis=0, keepdims=True)
            first = jnp.min(jnp.where(g == mx, brow, nrow), axis=0, keepdims=True)
            pick = brow == first
            sel = jnp.logical_or(sel, pick)
            g = jnp.where(pick, -jnp.inf, g)
        chosen = jnp.logical_and(sel, past)
        bias_t = jnp.where(chosen, 0.0, NEG_BIG)
        bias_t = jnp.concatenate([bias_t, jnp.zeros((LANES - nrow, qt), F32)], axis=0)
        qext = jnp.concatenate([q, bias_t.T.astype(BF16)], axis=1)

        prev_sel = jnp.max(jnp.where(jnp.logical_and(chosen, brow == 2 * a), 1.0, 0.0),
                           axis=0, keepdims=True) > 0.5
        allowed = jnp.logical_or(causal_self, jnp.logical_and(cross, prev_sel))
        s_diag = jnp.where(allowed, _dot_nt(k_ref[pl.ds(row0, qt), :], q), NEG_BIG)
        s_ref[0] = s_diag
        mx_diag = jnp.max(s_diag, axis=0, keepdims=True)

        def absorb(m, l, acc, slot, mx, blk_idx):
            c0 = pl.multiple_of(blk_idx * qt, qt)
            m_new = jnp.maximum(m, mx)
            alpha = jnp.exp2(m - m_new)
            p = jnp.exp2(s_ref[slot] - m_new)
            l = alpha * l + jnp.sum(p, axis=0, keepdims=True)
            acc = alpha * acc + jnp.dot(vt_ref[:, pl.ds(c0, qt)], p.astype(BF16),
                                        preferred_element_type=F32)
            return m_new, l, acc

        def kv_step(carry, b, slot):
            m, l, acc, mx_cur, blk_cur = carry
            c_next = pl.multiple_of(b * qt, qt)
            s_next = _dot_nt(kext_ref[pl.ds(c_next, qt), :], qext)
            s_ref[1 - slot] = s_next
            mx_next = jnp.max(s_next, axis=0, keepdims=True)
            m, l, acc = absorb(m, l, acc, slot, mx_cur, blk_cur)
            return m, l, acc, mx_next, b

        def kv_pair(t, carry):
            return kv_step(kv_step(carry, 2 * t, 0), 2 * t + 1, 1)

        init = (jnp.full((1, qt), NEG_BIG, F32), jnp.zeros((1, qt), F32),
                jnp.zeros((HEAD_DIM, qt), F32), mx_diag, a)
        carry = lax.fori_loop(0, a // 2, kv_pair, init)

        def odd_tail(carry):
            m, l, acc, mx_last, blk_last = kv_step(carry, a - 1, 0)
            return absorb(m, l, acc, 1, mx_last, blk_last)

        def even_tail(carry):
            m, l, acc, mx_last, blk_last = carry
            return absorb(m, l, acc, 0, mx_last, blk_last)

        m, l, acc = lax.cond(a % 2 == 1, odd_tail, even_tail, carry)
        o_ref[pl.ds(row0, qt), :] = (acc * (1.0 / l)).T.astype(o_ref.dtype)
        return 0

    lax.fori_loop(0, nsb, q_super, 0)


def _moba(qkv, bsz, seq):
    h = ATTN_HEADS
    kern = functools.partial(_moba_kernel, seq=seq)
    return pl.pallas_call(
        kern,
        out_shape=jax.ShapeDtypeStruct((bsz, seq, h * HEAD_DIM), BF16),
        grid=(bsz, h),
        in_specs=[pl.BlockSpec((None, seq, HEAD_DIM), lambda b, hh: (b, 0, hh)),
                  pl.BlockSpec((None, seq, HEAD_DIM), lambda b, hh: (b, 0, h + hh)),
                  pl.BlockSpec((None, seq, HEAD_DIM), lambda b, hh: (b, 0, 2 * h + hh))],
        out_specs=pl.BlockSpec((None, seq, HEAD_DIM), lambda b, hh: (b, 0, hh)),
        scratch_shapes=[pltpu.VMEM((seq, 2 * HEAD_DIM), BF16),
                        pltpu.VMEM((HEAD_DIM, seq), BF16),
                        pltpu.VMEM((2, 2 * MOBA_BLOCK, 2 * MOBA_BLOCK), F32)],
        compiler_params=_cparams(("parallel", "parallel")),
        name="moba_attention",
    )(qkv, qkv, qkv)


def _lru_kernel(x_ref, y_ref, cw_ref, cb_ref, w_ref, bias_ref, lam_ref, o_ref,
                win_ref, a_ref, b_ref, h_ref, *, ts):
    t_idx = pl.program_id(2)

    @pl.when(t_idx == 0)
    def _():
        win_ref[0:SUBLANES, :] = jnp.zeros((SUBLANES, win_ref.shape[1]), F32)
        h_ref[...] = jnp.zeros_like(h_ref)

    win_ref[SUBLANES:, :] = x_ref[...]
    xc = cb_ref[...] + cw_ref[CONV_WIDTH - 1:CONV_WIDTH, :] * x_ref[...]
    for k in range(CONV_WIDTH - 1):
        shift = CONV_WIDTH - 1 - k
        xc = xc + cw_ref[k:k + 1, :] * win_ref[SUBLANES - shift:SUBLANES - shift + ts, :]
    win_ref[0:SUBLANES, :] = win_ref[ts:ts + SUBLANES, :]

    gates = jnp.dot(xc.astype(BF16), w_ref[...], preferred_element_type=F32) + bias_ref[...]
    cblk = xc.shape[1]
    rt = _sigmoid(gates[:, :cblk])
    it = _sigmoid(gates[:, cblk:])
    neg_lam = -lam_ref[...]
    softplus = jnp.maximum(neg_lam, 0.0) + jnp.log1p(jnp.exp(-jnp.abs(neg_lam)))
    log_a = (-LRU_C) * rt * softplus
    a = jnp.exp(log_a)
    a_ref[...] = a
    b_ref[...] = jnp.sqrt(-jnp.tanh(log_a) * (a * a + 1.0)) * (it * xc)

    row = lax.broadcasted_iota(jnp.int32, (SUBLANES, cblk), 0)

    def group(gi, h_in):
        r0 = pl.multiple_of(gi * SUBLANES, SUBLANES)
        a = a_ref[pl.ds(r0, SUBLANES), :]
        b = b_ref[pl.ds(r0, SUBLANES), :]
        for d in (1, 2, 4):
            keep = row >= d
            a_sh = jnp.where(keep, pltpu.roll(a, d, axis=0), 1.0)
            b_sh = jnp.where(keep, pltpu.roll(b, d, axis=0), 0.0)
            b = a * b_sh + b
            a = a * a_sh
        hs = b + a * h_in
        o_ref[pl.ds(r0, SUBLANES), :] = (hs * y_ref[pl.ds(r0, SUBLANES), :]).astype(o_ref.dtype)
        return jnp.broadcast_to(hs[SUBLANES - 1:SUBLANES, :], (SUBLANES, cblk))

    h_ref[...] = lax.fori_loop(0, ts // SUBLANES, group, h_ref[...], unroll=4)


def _lru(xr, yr, conv_w, conv_b, wa, ba, wx, bx, lam, bsz, seq, ts=1024):
    r = xr.shape[-1]
    nblk = LRU_BLOCKS
    cblk = r // nblk
    w_cat = jnp.concatenate([wa, wx], axis=-1).astype(BF16)
    b_cat = jnp.concatenate([ba.reshape(nblk, 1, cblk), bx.reshape(nblk, 1, cblk)], axis=-1).astype(F32)
    kern = functools.partial(_lru_kernel, ts=ts)
    chan = lambda b, n, t: (0, n)
    return pl.pallas_call(
        kern,
        out_shape=jax.ShapeDtypeStruct((bsz, seq, r), BF16),
        grid=(bsz, nblk, seq // ts),
        in_specs=[pl.BlockSpec((None, ts, cblk), lambda b, n, t: (b, t, n)),
                  pl.BlockSpec((None, ts, cblk), lambda b, n, t: (b, t, n)),
                  pl.BlockSpec((CONV_WIDTH, cblk), chan),
                  pl.BlockSpec((1, cblk), chan),
                  pl.BlockSpec((None, cblk, 2 * cblk), lambda b, n, t: (n, 0, 0)),
                  pl.BlockSpec((None, 1, 2 * cblk), lambda b, n, t: (n, 0, 0)),
                  pl.BlockSpec((1, cblk), chan)],
        out_specs=pl.BlockSpec((None, ts, cblk), lambda b, n, t: (b, t, n)),
        scratch_shapes=[pltpu.VMEM((ts + SUBLANES, cblk), F32),
                        pltpu.VMEM((ts, cblk), F32),
                        pltpu.VMEM((ts, cblk), F32),
                        pltpu.VMEM((SUBLANES, cblk), F32)],
        compiler_params=_cparams(("parallel", "parallel", "arbitrary")),
        name="conv_rglru",
    )(xr, yr, conv_w.astype(F32), conv_b.reshape(1, r).astype(F32), w_cat, b_cat,
      lam.reshape(1, r).astype(F32))


def _merge_kernel(o_ref, hy_ref, wa_ref, wr_ref, ga_ref, gr_ref, out_ref):
    ya = jnp.dot(o_ref[...], wa_ref[...], preferred_element_type=F32)
    yr = jnp.dot(hy_ref[...], wr_ref[...], preferred_element_type=F32)
    out_ref[...] = (ga_ref[...] * ya + gr_ref[...] * yr).astype(out_ref.dtype)


def _merge(o, hy, wa, wr, gates, tm=1024, tn=512):
    t, k = o.shape
    kr = hy.shape[1]
    n = wa.shape[1]
    nj = n // tn
    return pl.pallas_call(
        _merge_kernel,
        out_shape=jax.ShapeDtypeStruct((t, n), BF16),
        grid=(t // tm, nj),
        in_specs=[pl.BlockSpec((tm, k), lambda i, j: (i, 0)),
                  pl.BlockSpec((tm, kr), lambda i, j: (i, 0)),
                  pl.BlockSpec((k, tn), lambda i, j: (0, j)),
                  pl.BlockSpec((kr, tn), lambda i, j: (0, j)),
                  pl.BlockSpec((tm, tn), lambda i, j: (i, j)),
                  pl.BlockSpec((tm, tn), lambda i, j: (i, nj + j))],
        out_specs=pl.BlockSpec((tm, tn), lambda i, j: (i, j)),
        compiler_params=_cparams(("parallel", "parallel")),
        name="branch_merge",
    )(o, hy, wa, wr, gates, gates)


def _pack_halves(x):
    bits = pltpu.bitcast(x.astype(F32), jnp.uint32)
    half = x.shape[1] // 2
    return (bits[:, :half] >> 16) | bits[:, half:]


def _unpack_halves(w):
    lo = pltpu.bitcast(w << 16, F32).astype(BF16)
    hi = pltpu.bitcast(w & jnp.uint32(0xFFFF0000), F32).astype(BF16)
    return jnp.concatenate([lo, hi], axis=1)


def _outproj_kernel(m_ref, wo_ref, x_ref, g_ref, rw_ref, rb_ref, x1_ref, hf_ref, lg_ref):
    x1 = x_ref[...] + jnp.dot(m_ref[...], wo_ref[...], preferred_element_type=F32)
    x1_ref[...] = x1
    ms = jnp.mean(x1 * x1, axis=-1, keepdims=True)
    hf = (x1 * lax.rsqrt(ms + RMS_EPS) * g_ref[...]).astype(BF16)
    hf_ref[...] = _pack_halves(hf)
    lg_ref[...] = jnp.dot(hf, rw_ref[...], preferred_element_type=F32) + rb_ref[...]


def _outproj(merged, wo, x, g, rw, rb, tm=512):
    t, d = x.shape
    ne = rw.shape[1]
    rw_p = jnp.zeros((d, LANES), BF16).at[:, :ne].set(rw.astype(BF16))
    rb_p = jnp.zeros((1, LANES), F32).at[0, :ne].set(rb.astype(F32))
    row = lambda i: (i, 0)
    fixed = lambda i: (0, 0)
    return pl.pallas_call(
        _outproj_kernel,
        out_shape=(jax.ShapeDtypeStruct((t, d), F32),
                   jax.ShapeDtypeStruct((t, d // 2), jnp.uint32),
                   jax.ShapeDtypeStruct((t, LANES), F32)),
        grid=(t // tm,),
        in_specs=[pl.BlockSpec((tm, d), row),
                  pl.BlockSpec((d, d), fixed),
                  pl.BlockSpec((tm, d), row),
                  pl.BlockSpec((1, d), fixed),
                  pl.BlockSpec((d, LANES), fixed),
                  pl.BlockSpec((1, LANES), fixed)],
        out_specs=(pl.BlockSpec((tm, d), row),
                   pl.BlockSpec((tm, d // 2), row),
                   pl.BlockSpec((tm, LANES), row)),
        compiler_params=_cparams(("parallel",)),
        name="outproj_norm_router",
    )(merged, wo, x, g.reshape(1, d).astype(F32), rw_p, rb_p)


def _expert_changed(be_ref, r):
    return jnp.logical_or(r == 0, be_ref[r] != be_ref[jnp.maximum(r - 1, 0)])


def _expert_up_kernel(be_ref, nu_ref, xs_ref, wg_ref, bg_ref, wu_ref, bu_ref, act_ref,
                      wgb_ref, wub_ref):
    r = pl.program_id(1)
    used = r < nu_ref[0]

    @pl.when(jnp.logical_and(used, _expert_changed(be_ref, r)))
    def _():
        wgb_ref[...] = wg_ref[...].astype(BF16)
        wub_ref[...] = wu_ref[...].astype(BF16)

    @pl.when(used)
    def _():
        xs = _unpack_halves(xs_ref[...])
        glu = jnp.minimum(jnp.dot(xs, wgb_ref[...], preferred_element_type=F32) + bg_ref[...],
                          SWIGLU_LIMIT)
        lin = jnp.clip(jnp.dot(xs, wub_ref[...], preferred_element_type=F32) + bu_ref[...],
                       -SWIGLU_LIMIT, SWIGLU_LIMIT)
        act_ref[...] = (glu * _sigmoid(SWIGLU_ALPHA * glu) * (lin + 1.0)).astype(act_ref.dtype)

    @pl.when(jnp.logical_not(used))
    def _():
        act_ref[...] = jnp.zeros_like(act_ref)


def _expert_down_kernel(be_ref, nu_ref, act_ref, wd_ref, bd_ref, y_ref, wdb_ref):
    r = pl.program_id(1)
    used = r < nu_ref[0]

    @pl.when(jnp.logical_and(used, _expert_changed(be_ref, r)))
    def _():
        wdb_ref[...] = wd_ref[...].astype(BF16)

    @pl.when(used)
    def _():
        y_ref[...] = jnp.dot(act_ref[...], wdb_ref[...], preferred_element_type=F32) + bd_ref[...]

    @pl.when(jnp.logical_not(used))
    def _():
        y_ref[...] = jnp.zeros_like(y_ref)


def _experts(xs, block_expert, n_used, wg, bg, wu, bu, wd, bd, tf=1024, tn=1024):
    p = xs.shape[0]
    ne, d, dff = wg.shape
    nblk = p // MOE_BLOCK

    def rr(r, nu):
        return jnp.minimum(r, nu[0] - 1)

    wspec = lambda k, t: pl.BlockSpec((None, k, t), lambda c, r, be, nu: (be[rr(r, nu)], 0, c))
    bspec = lambda t: pl.BlockSpec((None, 1, t), lambda c, r, be, nu: (be[rr(r, nu)], 0, c))
    rows = lambda k: pl.BlockSpec((MOE_BLOCK, k), lambda c, r, be, nu: (rr(r, nu), 0))

    act = pl.pallas_call(
        _expert_up_kernel,
        out_shape=jax.ShapeDtypeStruct((p, dff), BF16),
        grid_spec=pltpu.PrefetchScalarGridSpec(
            num_scalar_prefetch=2,
            grid=(dff // tf, nblk),
            in_specs=[rows(d // 2), wspec(d, tf), bspec(tf), wspec(d, tf), bspec(tf)],
            out_specs=pl.BlockSpec((MOE_BLOCK, tf), lambda c, r, be, nu: (r, c)),
            scratch_shapes=[pltpu.VMEM((d, tf), BF16), pltpu.VMEM((d, tf), BF16)]),
        compiler_params=_cparams(("arbitrary", "arbitrary")),
        name="moe_up",
    )(block_expert, n_used, xs, wg, bg.reshape(ne, 1, dff).astype(F32), wu,
      bu.reshape(ne, 1, dff).astype(F32))

    return pl.pallas_call(
        _expert_down_kernel,
        out_shape=jax.ShapeDtypeStruct((p, d), F32),
        grid_spec=pltpu.PrefetchScalarGridSpec(
            num_scalar_prefetch=2,
            grid=(d // tn, nblk),
            in_specs=[rows(dff), wspec(dff, tn), bspec(tn)],
            out_specs=pl.BlockSpec((MOE_BLOCK, tn), lambda c, r, be, nu: (r, c)),
            scratch_shapes=[pltpu.VMEM((dff, tn), BF16)]),
        compiler_params=_cparams(("arbitrary", "arbitrary")),
        name="moe_down",
    )(block_expert, n_used, act, wd, bd.reshape(ne, 1, d).astype(F32))


def _combine_kernel(dest_ref, x1_ref, gw_ref, g_ref, y_hbm, o_ref, buf_ref, sem_ref,
                    *, tt, ct, norm):
    base = pl.program_id(0) * (tt * MOE_TOPK)
    nch = tt // ct

    def row_copy(row, slot, k, tok):
        return pltpu.make_async_copy(y_hbm.at[pl.ds(row, 1), :],
                                     buf_ref.at[slot, k, pl.ds(tok, 1), :], sem_ref.at[slot])

    def issue(c):
        def body(tok, _):
            for k in range(MOE_TOPK):
                row = dest_ref[base + (c * ct + tok) * MOE_TOPK + k]
                row_copy(row, c % 2, k, tok).start()
            return 0
        lax.fori_loop(0, ct, body, 0, unroll=2)

    def wait(c):
        for k in range(MOE_TOPK):
            pltpu.make_async_copy(y_hbm.at[pl.ds(0, ct), :], buf_ref.at[c % 2, k],
                                  sem_ref.at[c % 2]).wait()

    issue(0)
    for c in range(nch):
        if c + 1 < nch:
            issue(c + 1)
        wait(c)
        rows = pl.ds(c * ct, ct)
        acc = x1_ref[rows, :]
        gw = gw_ref[rows, :]
        for k in range(MOE_TOPK):
            acc = acc + gw[:, k:k + 1] * buf_ref[c % 2, k]
        if norm:
            ms = jnp.mean(acc * acc, axis=-1, keepdims=True)
            acc = acc * lax.rsqrt(ms + RMS_EPS) * g_ref[...]
        o_ref[rows, :] = acc


def _combine(x1, y, dest, gate_w, g, norm, tt=256, ct=64):
    t, d = x1.shape
    kern = functools.partial(_combine_kernel, tt=tt, ct=ct, norm=norm)
    return pl.pallas_call(
        kern,
        out_shape=jax.ShapeDtypeStruct((t, d), F32),
        grid_spec=pltpu.PrefetchScalarGridSpec(
            num_scalar_prefetch=1,
            grid=(t // tt,),
            in_specs=[pl.BlockSpec((tt, d), lambda i, dref: (i, 0)),
                      pl.BlockSpec((tt, MOE_TOPK), lambda i, dref: (i, 0)),
                      pl.BlockSpec((1, d), lambda i, dref: (0, 0)),
                      pl.BlockSpec(memory_space=pl.ANY)],
            out_specs=pl.BlockSpec((tt, d), lambda i, dref: (i, 0)),
            scratch_shapes=[pltpu.VMEM((2, MOE_TOPK, ct, d), F32),
                            pltpu.SemaphoreType.DMA((2,))]),
        compiler_params=_cparams(("arbitrary",)),
        name="moe_combine",
    )(dest.reshape(t * MOE_TOPK), x1, gate_w, g.reshape(1, d).astype(F32), y)


def _route_kernel(lg_ref, out_ref, cnt_ref, base_ref, *, n_experts, tm):
    @pl.when(pl.program_id(0) == 0)
    def _():
        base_ref[...] = jnp.zeros_like(base_ref)

    lane = lax.broadcasted_iota(jnp.int32, (tm, LANES), 1)
    lg = jnp.where(lane < n_experts, lg_ref[...], -jnp.inf)
    vals, ids = [], []
    onehot = jnp.zeros((tm, LANES), jnp.bool_)
    for _k in range(MOE_TOPK):
        mx = jnp.max(lg, axis=1, keepdims=True)
        first = jnp.min(jnp.where(lg == mx, lane, LANES), axis=1, keepdims=True)
        pick = lane == first
        onehot = jnp.logical_or(onehot, pick)
        lg = jnp.where(pick, -jnp.inf, lg)
        vals.append(mx)
        ids.append(first)
    exps = [jnp.exp(v - vals[0]) for v in vals]
    denom = exps[0] + exps[1] + exps[2] + exps[3]

    oh = jnp.where(onehot, 1.0, 0.0)
    r_i = lax.broadcasted_iota(jnp.int32, (tm, tm), 0)
    c_i = lax.broadcasted_iota(jnp.int32, (tm, tm), 1)
    earlier = jnp.where(r_i > c_i, 1.0, 0.0).astype(BF16)
    before = base_ref[...] + jnp.dot(earlier, oh.astype(BF16), preferred_element_type=F32)
    base_ref[...] = base_ref[...] + jnp.sum(oh, axis=0, keepdims=True)
    cnt_ref[...] = base_ref[...]

    out = jnp.zeros((tm, LANES), F32)
    for k in range(MOE_TOPK):
        rank = jnp.sum(jnp.where(lane == ids[k], before, 0.0), axis=1, keepdims=True)
        out = jnp.where(lane == k, exps[k] / denom, out)
        out = jnp.where(lane == MOE_TOPK + k, ids[k].astype(F32), out)
        out = jnp.where(lane == 2 * MOE_TOPK + k, rank, out)
    out_ref[...] = out


def _route(logits, n_experts, tm=512):
    t = logits.shape[0]
    kern = functools.partial(_route_kernel, n_experts=n_experts, tm=tm)
    packed, counts = pl.pallas_call(
        kern,
        out_shape=(jax.ShapeDtypeStruct((t, LANES), F32), jax.ShapeDtypeStruct((1, LANES), F32)),
        grid=(t // tm,),
        in_specs=[pl.BlockSpec((tm, LANES), lambda i: (i, 0))],
        out_specs=(pl.BlockSpec((tm, LANES), lambda i: (i, 0)),
                   pl.BlockSpec((1, LANES), lambda i: (0, 0))),
        scratch_shapes=[pltpu.VMEM((1, LANES), F32)],
        compiler_params=_cparams(("arbitrary",)),
        name="moe_route",
    )(logits)
    blk = MOE_BLOCK
    n_asg = t * MOE_TOPK
    p = (-(-n_asg // blk)) * blk + n_experts * blk
    cnt = counts[0, :n_experts].astype(jnp.int32)
    padded = (cnt + blk - 1) // blk * blk
    pad_end = jnp.cumsum(padded)
    pad_start = pad_end - padded
    row_starts = jnp.arange(p // blk, dtype=jnp.int32) * blk
    block_expert = jnp.minimum(jnp.sum(pad_end[None, :] <= row_starts[:, None], axis=1),
                               n_experts - 1).astype(jnp.int32)
    n_used = (pad_end[-1] // blk).astype(jnp.int32).reshape(1)
    gate_w = packed[:, :MOE_TOPK]
    ids = packed[:, MOE_TOPK:2 * MOE_TOPK].astype(jnp.int32)
    rank = packed[:, 2 * MOE_TOPK:3 * MOE_TOPK].astype(jnp.int32)
    sel = ids[:, :, None] == jnp.arange(n_experts, dtype=jnp.int32)[None, None, :]
    dest = rank + jnp.sum(jnp.where(sel, pad_start[None, None, :], 0), axis=-1)
    fill_start = jnp.minimum((pad_start + cnt) // SUBLANES * SUBLANES, p - blk).astype(jnp.int32)
    return gate_w, dest, block_expert, n_used, fill_start, p


def _dispatch_kernel(dest_ref, fill_ref, nu_ref, hf_hbm, xs_hbm, zero_ref, sem_ref,
                     *, tt, n_experts, nblk):
    i = pl.program_id(0)

    def zero_fill(start):
        return pltpu.make_async_copy(zero_ref, xs_hbm.at[pl.ds(start, MOE_BLOCK), :],
                                     sem_ref.at[0])

    @pl.when(i == 0)
    def _():
        zero_ref[...] = jnp.zeros_like(zero_ref)
        for e in range(n_experts):
            fill = zero_fill(pl.multiple_of(fill_ref[e], SUBLANES))
            fill.start()
            fill.wait()

        def fill_block(r, _):
            fill = zero_fill(pl.multiple_of(r * MOE_BLOCK, MOE_BLOCK))
            fill.start()
            fill.wait()
            return 0

        lax.fori_loop(nu_ref[0], nblk, fill_block, 0)

    base = i * tt

    def body(tok, _):
        for k in range(MOE_TOPK):
            row = dest_ref[(base + tok) * MOE_TOPK + k]
            pltpu.make_async_copy(hf_hbm.at[pl.ds(base + tok, 1), :],
                                  xs_hbm.at[pl.ds(row, 1), :], sem_ref.at[1]).start()
        return 0

    lax.fori_loop(0, tt, body, 0, unroll=2)
    pltpu.make_async_copy(hf_hbm.at[pl.ds(0, tt * MOE_TOPK), :],
                          xs_hbm.at[pl.ds(0, tt * MOE_TOPK), :], sem_ref.at[1]).wait()


def _dispatch(hfp, dest, fill_start, n_used, p, n_experts, tt=512):
    t, w = hfp.shape
    kern = functools.partial(_dispatch_kernel, tt=tt, n_experts=n_experts, nblk=p // MOE_BLOCK)
    return pl.pallas_call(
        kern,
        out_shape=jax.ShapeDtypeStruct((p, w), hfp.dtype),
        grid_spec=pltpu.PrefetchScalarGridSpec(
            num_scalar_prefetch=3,
            grid=(t // tt,),
            in_specs=[pl.BlockSpec(memory_space=pl.ANY)],
            out_specs=pl.BlockSpec(memory_space=pl.ANY),
            scratch_shapes=[pltpu.VMEM((MOE_BLOCK, w), hfp.dtype),
                            pltpu.SemaphoreType.DMA((2,))]),
        compiler_params=_cparams(("arbitrary",)),
        name="moe_dispatch",
    )(dest.reshape(t * MOE_TOPK), fill_start, n_used, hfp)


def _layer(x2, bsz, seq, norm_mix_g, w_in, conv_w, conv_b, lru_wa, lru_ba, lru_wx, lru_bx,
           lru_lambda, w_attn_branch, w_rnn_branch, w_out, norm_ffn_g, router_w, router_b,
           w_gate, b_gate, w_up, b_up, w_down, b_down):
    t, d = x2.shape
    aw = ATTN_HEADS * HEAD_DIM
    r = lru_lambda.shape[0]
    ne = router_w.shape[1]

    hm = _rmsnorm(x2, norm_mix_g, BF16)
    w_in_b = w_in.astype(BF16)
    qkv = _proj(hm, w_in_b, 0, 3 * aw, BF16, "scale_head",
                n_scaled_tiles=aw // 1024, scale=HEAD_DIM ** -0.5 * LOG2E)
    xr = _proj(hm, w_in_b, 3 * aw, r, F32)
    yr = _proj(hm, w_in_b, 3 * aw + r, r, F32, "gelu")
    gates = _proj(hm, w_in_b, 3 * aw + 2 * r, 2 * d, F32, "sigmoid")

    o = _moba(qkv.reshape(bsz, seq, 3 * aw), bsz, seq).reshape(t, aw)
    hy = _lru(xr.reshape(bsz, seq, r), yr.reshape(bsz, seq, r), conv_w, conv_b,
              lru_wa, lru_ba, lru_wx, lru_bx, lru_lambda, bsz, seq).reshape(t, r)

    merged = _merge(o, hy, w_attn_branch.astype(BF16), w_rnn_branch.astype(BF16), gates)
    x1, hf, logits = _outproj(merged, w_out.astype(BF16), x2, norm_ffn_g, router_w, router_b)

    gate_w, dest, block_expert, n_used, fill_start, p = _route(logits, ne)
    xs = _dispatch(hf, dest, fill_start, n_used, p, ne)
    y = _experts(xs, block_expert, n_used, w_gate, b_gate, w_up, b_up, w_down, b_down)
    return x1, y, dest, gate_w


def kernel(x, norm_mix_g, w_in, conv_w, conv_b, lru_wa, lru_ba, lru_wx, lru_bx, lru_lambda, w_attn_branch, w_rnn_branch, w_out, norm_ffn_g, router_w, router_b, w_gate, b_gate, w_up, b_up, w_down, b_down, norm_final_g):
    bsz, seq, d = x.shape
    depth = w_in.shape[0]
    assert seq % (2 * MOBA_BLOCK) == 0 and d % LANES == 0
    x2 = x.reshape(bsz * seq, d)
    for l in range(depth):
        x1, y, dest, gate_w = _layer(
            x2, bsz, seq, norm_mix_g[l], w_in[l], conv_w[l], conv_b[l], lru_wa[l], lru_ba[l],
            lru_wx[l], lru_bx[l], lru_lambda[l], w_attn_branch[l], w_rnn_branch[l], w_out[l],
            norm_ffn_g[l], router_w[l], router_b[l], w_gate[l], b_gate[l], w_up[l], b_up[l],
            w_down[l], b_down[l])
        last = l + 1 == depth
        x2 = _combine(x1, y, dest, gate_w, norm_final_g, norm=last)
    return x2.reshape(bsz, seq, d)
```

```python
import functools

import jax
import jax.numpy as jnp
from jax import lax
from jax.experimental import pallas as pl
from jax.experimental.pallas import tpu as pltpu

ATTN_HEADS = 16
HEAD_DIM = 128
MOBA_BLOCK = 256
MOBA_TOPK = 3
LRU_BLOCKS = 8
CONV_WIDTH = 4
LRU_C = 8.0
MOE_TOPK = 4
SWIGLU_ALPHA = 1.702
SWIGLU_LIMIT = 7.0
MOE_BLOCK = 512
RMS_EPS = 1e-6

LANES = 128
SUBLANES = 8
VMEM_LIMIT = 56 * 1024 * 1024
LOG2E = 1.4426950408889634
NEG_BIG = -1e30

BF16 = jnp.bfloat16
F32 = jnp.float32


def _cparams(sem):
    return pltpu.CompilerParams(dimension_semantics=sem, vmem_limit_bytes=VMEM_LIMIT)


def _sigmoid(x):
    return 1.0 / (1.0 + jnp.exp(-x))


def _gelu_tanh(x):
    c = 0.7978845608028654
    return 0.5 * x * (1.0 + jnp.tanh(c * (x + 0.044715 * (x * x * x))))


def _rmsnorm_kernel(x_ref, g_ref, o_ref):
    x = x_ref[...]
    ms = jnp.mean(x * x, axis=-1, keepdims=True)
    o_ref[...] = (x * lax.rsqrt(ms + RMS_EPS) * g_ref[...]).astype(o_ref.dtype)


def _rmsnorm(x, g, out_dtype, tm=1024):
    t, d = x.shape
    return pl.pallas_call(
        _rmsnorm_kernel,
        out_shape=jax.ShapeDtypeStruct((t, d), out_dtype),
        grid=(t // tm,),
        in_specs=[pl.BlockSpec((tm, d), lambda i: (i, 0)),
                  pl.BlockSpec((1, d), lambda i: (0, 0))],
        out_specs=pl.BlockSpec((tm, d), lambda i: (i, 0)),
        compiler_params=_cparams(("parallel",)),
        name="rmsnorm",
    )(x, g.reshape(1, d).astype(F32))


def _proj_kernel(a_ref, w_ref, o_ref, *, epilogue, n_scaled_tiles, scale):
    acc = jnp.dot(a_ref[...], w_ref[...], preferred_element_type=F32)
    if epilogue == "scale_head":
        s = jnp.where(pl.program_id(1) < n_scaled_tiles, scale, 1.0).astype(F32)
        acc = acc * s
    elif epilogue == "gelu":
        acc = _gelu_tanh(acc)
    elif epilogue == "sigmoid":
        acc = _sigmoid(acc)
    o_ref[...] = acc.astype(o_ref.dtype)


def _proj(a, w, col0, n, out_dtype, epilogue="none", n_scaled_tiles=0, scale=1.0,
          tm=1024, tn=1024):
    t, k = a.shape
    c0 = col0 // tn
    kern = functools.partial(_proj_kernel, epilogue=epilogue,
                             n_scaled_tiles=n_scaled_tiles, scale=scale)
    return pl.pallas_call(
        kern,
        out_shape=jax.ShapeDtypeStruct((t, n), out_dtype),
        grid=(t // tm, n // tn),
        in_specs=[pl.BlockSpec((tm, k), lambda i, j: (i, 0)),
                  pl.BlockSpec((k, tn), lambda i, j: (0, c0 + j))],
        out_specs=pl.BlockSpec((tm, tn), lambda i, j: (i, j)),
        compiler_params=_cparams(("parallel", "parallel")),
        name="proj_" + epilogue,
    )(a, w)


def _dot_nt(a, b):
    return lax.dot_general(a, b, (((1,), (1,)), ((), ())), preferred_element_type=F32)


def _moba_kernel(q_ref, k_ref, v_ref, o_ref, kext_ref, vt_ref, s_ref, *, seq):
    blk = MOBA_BLOCK
    qt = 2 * blk
    nsb = seq // qt

    kext_ref[:, :HEAD_DIM] = k_ref[...]
    r_blk = lax.broadcasted_iota(jnp.int32, (seq, LANES), 0) // blk
    c_id = lax.broadcasted_iota(jnp.int32, (seq, LANES), 1)
    kext_ref[:, HEAD_DIM:] = jnp.where(r_blk == c_id, 1.0, 0.0).astype(BF16)
    vt_ref[...] = v_ref[...].T

    a_row = lax.broadcasted_iota(jnp.int32, (LANES, seq), 0)
    a_col = lax.broadcasted_iota(jnp.int32, (LANES, seq), 1) // blk
    avg = jnp.where(a_row == a_col, 1.0 / blk, 0.0).astype(BF16)
    kmean = jnp.dot(avg, k_ref[...], preferred_element_type=F32).astype(BF16)

    nrow = 32
    brow = lax.broadcasted_iota(jnp.int32, (nrow, qt), 0)
    second = lax.broadcasted_iota(jnp.int32, (nrow, qt), 1) >= blk
    key_i = lax.broadcasted_iota(jnp.int32, (qt, qt), 0)
    qry_i = lax.broadcasted_iota(jnp.int32, (qt, qt), 1)
    same_blk = (key_i >= blk) == (qry_i >= blk)
    causal_self = jnp.logical_and(same_blk, key_i <= qry_i)
    cross = jnp.logical_and(key_i < blk, qry_i >= blk)

    def q_super(a, _):
        row0 = pl.multiple_of(a * qt, qt)
        q = q_ref[pl.ds(row0, qt), :]
        gate = _dot_nt(kmean, q)[:nrow, :]
        qblk = 2 * a + second.astype(jnp.int32)
        past = brow < qblk
        g = jnp.where(past, gate, -jnp.inf)
        sel = jnp.zeros((nrow, qt), jnp.bool_)
        for _k in range(MOBA_TOPK):
            mx = jnp.max(g, axis=0, keepdims=True)
            first = jnp.min(jnp.where(g == mx, brow, nrow), axis=0, keepdims=True)
            pick = brow == first
            sel = jnp.logical_or(sel, pick)
            g = jnp.where(pick, -jnp.inf, g)
        chosen = jnp.logical_and(sel, past)
        bias_t = jnp.where(chosen, 0.0, NEG_BIG)
        bias_t = jnp.concatenate([bias_t, jnp.zeros((LANES - nrow, qt), F32)], axis=0)
        qext = jnp.concatenate([q, bias_t.T.astype(BF16)], axis=1)

        prev_sel = jnp.max(jnp.where(jnp.logical_and(chosen, brow == 2 * a), 1.0, 0.0),
                           axis=0, keepdims=True) > 0.5
        allowed = jnp.logical_or(causal_self, jnp.logical_and(cross, prev_sel))
        s_diag = jnp.where(allowed, _dot_nt(k_ref[pl.ds(row0, qt), :], q), NEG_BIG)
        s_ref[0] = s_diag
        mx_diag = jnp.max(s_diag, axis=0, keepdims=True)

        def absorb(m, l, acc, slot, mx, blk_idx):
            c0 = pl.multiple_of(blk_idx * qt, qt)
            m_new = jnp.maximum(m, mx)
            alpha = jnp.exp2(m - m_new)
            p = jnp.exp2(s_ref[slot] - m_new)
            l = alpha * l + jnp.sum(p, axis=0, keepdims=True)
            acc = alpha * acc + jnp.dot(vt_ref[:, pl.ds(c0, qt)], p.astype(BF16),
                                        preferred_element_type=F32)
            return m_new, l, acc

        def kv_step(carry, b, slot):
            m, l, acc, mx_cur, blk_cur = carry
            c_next = pl.multiple_of(b * qt, qt)
            s_next = _dot_nt(kext_ref[pl.ds(c_next, qt), :], qext)
            s_ref[1 - slot] = s_next
            mx_next = jnp.max(s_next, axis=0, keepdims=True)
            m, l, acc = absorb(m, l, acc, slot, mx_cur, blk_cur)
            return m, l, acc, mx_next, b

        def kv_pair(t, carry):
            return kv_step(kv_step(carry, 2 * t, 0), 2 * t + 1, 1)

        init = (jnp.full((1, qt), NEG_BIG, F32), jnp.zeros((1, qt), F32),
                jnp.zeros((HEAD_DIM, qt), F32), mx_diag, a)
        carry = lax.fori_loop(0, a // 2, kv_pair, init)

        def odd_tail(carry):
            m, l, acc, mx_last, blk_last = kv_step(carry, a - 1, 0)
            return absorb(m, l, acc, 1, mx_last, blk_last)

        def even_tail(carry):
            m, l, acc, mx_last, blk_last = carry
            return absorb(m, l, acc, 0, mx_last, blk_last)

        m, l, acc = lax.cond(a % 2 == 1, odd_tail, even_tail, carry)
        o_ref[pl.ds(row0, qt), :] = (acc * (1.0 / l)).T.astype(o_ref.dtype)
        return 0

    lax.fori_loop(0, nsb, q_super, 0)


def _moba(qkv, bsz, seq):
    h = ATTN_HEADS
    kern = functools.partial(_moba_kernel, seq=seq)
    return pl.pallas_call(
        kern,
        out_shape=jax.ShapeDtypeStruct((bsz, seq, h * HEAD_DIM), BF16),
        grid=(bsz, h),
        in_specs=[pl.BlockSpec((None, seq, HEAD_DIM), lambda b, hh: (b, 0, hh)),
                  pl.BlockSpec((None, seq, HEAD_DIM), lambda b, hh: (b, 0, h + hh)),
                  pl.BlockSpec((None, seq, HEAD_DIM), lambda b, hh: (b, 0, 2 * h + hh))],
        out_specs=pl.BlockSpec((None, seq, HEAD_DIM), lambda b, hh: (b, 0, hh)),
        scratch_shapes=[pltpu.VMEM((seq, 2 * HEAD_DIM), BF16),
                        pltpu.VMEM((HEAD_DIM, seq), BF16),
                        pltpu.VMEM((2, 2 * MOBA_BLOCK, 2 * MOBA_BLOCK), F32)],
        compiler_params=_cparams(("parallel", "parallel")),
        name="moba_attention",
    )(qkv, qkv, qkv)


def _lru_kernel(x_ref, y_ref, cw_ref, cb_ref, w_ref, bias_ref, lam_ref, o_ref,
                win_ref, a_ref, b_ref, h_ref, *, ts):
    t_idx = pl.program_id(2)

    @pl.when(t_idx == 0)
    def _():
        win_ref[0:SUBLANES, :] = jnp.zeros((SUBLANES, win_ref.shape[1]), F32)
        h_ref[...] = jnp.zeros_like(h_ref)

    win_ref[SUBLANES:, :] = x_ref[...]
    xc = cb_ref[...] + cw_ref[CONV_WIDTH - 1:CONV_WIDTH, :] * x_ref[...]
    for k in range(CONV_WIDTH - 1):
        shift = CONV_WIDTH - 1 - k
        xc = xc + cw_ref[k:k + 1, :] * win_ref[SUBLANES - shift:SUBLANES - shift + ts, :]
    win_ref[0:SUBLANES, :] = win_ref[ts:ts + SUBLANES, :]

    gates = jnp.dot(xc.astype(BF16), w_ref[...], preferred_element_type=F32) + bias_ref[...]
    cblk = xc.shape[1]
    rt = _sigmoid(gates[:, :cblk])
    it = _sigmoid(gates[:, cblk:])
    neg_lam = -lam_ref[...]
    softplus = jnp.maximum(neg_lam, 0.0) + jnp.log1p(jnp.exp(-jnp.abs(neg_lam)))
    log_a = (-LRU_C) * rt * softplus
    a = jnp.exp(log_a)
    a_ref[...] = a
    b_ref[...] = jnp.sqrt(-jnp.tanh(log_a) * (a * a + 1.0)) * (it * xc)

    row = lax.broadcasted_iota(jnp.int32, (SUBLANES, cblk), 0)

    def group(gi, h_in):
        r0 = pl.multiple_of(gi * SUBLANES, SUBLANES)
        a = a_ref[pl.ds(r0, SUBLANES), :]
        b = b_ref[pl.ds(r0, SUBLANES), :]
        for d in (1, 2, 4):
            keep = row >= d
            a_sh = jnp.where(keep, pltpu.roll(a, d, axis=0), 1.0)
            b_sh = jnp.where(keep, pltpu.roll(b, d, axis=0), 0.0)
            b = a * b_sh + b
            a = a * a_sh
        hs = b + a * h_in
        o_ref[pl.ds(r0, SUBLANES), :] = (hs * y_ref[pl.ds(r0, SUBLANES), :]).astype(o_ref.dtype)
        return jnp.broadcast_to(hs[SUBLANES - 1:SUBLANES, :], (SUBLANES, cblk))

    h_ref[...] = lax.fori_loop(0, ts // SUBLANES, group, h_ref[...], unroll=4)


def _lru(xr, yr, conv_w, conv_b, wa, ba, wx, bx, lam, bsz, seq, ts=1024):
    r = xr.shape[-1]
    nblk = LRU_BLOCKS
    cblk = r // nblk
    w_cat = jnp.concatenate([wa, wx], axis=-1).astype(BF16)
    b_cat = jnp.concatenate([ba.reshape(nblk, 1, cblk), bx.reshape(nblk, 1, cblk)], axis=-1).astype(F32)
    kern = functools.partial(_lru_kernel, ts=ts)
    chan = lambda b, n, t: (0, n)
    return pl.pallas_call(
        kern,
        out_shape=jax.ShapeDtypeStruct((bsz, seq, r), BF16),
        grid=(bsz, nblk, seq // ts),
        in_specs=[pl.BlockSpec((None, ts, cblk), lambda b, n, t: (b, t, n)),
                  pl.BlockSpec((None, ts, cblk), lambda b, n, t: (b, t, n)),
                  pl.BlockSpec((CONV_WIDTH, cblk), chan),
                  pl.BlockSpec((1, cblk), chan),
                  pl.BlockSpec((None, cblk, 2 * cblk), lambda b, n, t: (n, 0, 0)),
                  pl.BlockSpec((None, 1, 2 * cblk), lambda b, n, t: (n, 0, 0)),
                  pl.BlockSpec((1, cblk), chan)],
        out_specs=pl.BlockSpec((None, ts, cblk), lambda b, n, t: (b, t, n)),
        scratch_shapes=[pltpu.VMEM((ts + SUBLANES, cblk), F32),
                        pltpu.VMEM((ts, cblk), F32),
                        pltpu.VMEM((ts, cblk), F32),
                        pltpu.VMEM((SUBLANES, cblk), F32)],
        compiler_params=_cparams(("parallel", "parallel", "arbitrary")),
        name="conv_rglru",
    )(xr, yr, conv_w.astype(F32), conv_b.reshape(1, r).astype(F32), w_cat, b_cat,
      lam.reshape(1, r).astype(F32))


def _merge_kernel(o_ref, hy_ref, wa_ref, wr_ref, ga_ref, gr_ref, out_ref):
    ya = jnp.dot(o_ref[...], wa_ref[...], preferred_element_type=F32)
    yr = jnp.dot(hy_ref[...], wr_ref[...], preferred_element_type=F32)
    out_ref[...] = (ga_ref[...] * ya + gr_ref[...] * yr).astype(out_ref.dtype)


def _merge(o, hy, wa, wr, gates, tm=1024, tn=512):
    t, k = o.shape
    kr = hy.shape[1]
    n = wa.shape[1]
    nj = n // tn
    return pl.pallas_call(
        _merge_kernel,
        out_shape=jax.ShapeDtypeStruct((t, n), BF16),
        grid=(t // tm, nj),
        in_specs=[pl.BlockSpec((tm, k), lambda i, j: (i, 0)),
                  pl.BlockSpec((tm, kr), lambda i, j: (i, 0)),
                  pl.BlockSpec((k, tn), lambda i, j: (0, j)),
                  pl.BlockSpec((kr, tn), lambda i, j: (0, j)),
                  pl.BlockSpec((tm, tn), lambda i, j: (i, j)),
                  pl.BlockSpec((tm, tn), lambda i, j: (i, nj + j))],
        out_specs=pl.BlockSpec((tm, tn), lambda i, j: (i, j)),
        compiler_params=_cparams(("parallel", "parallel")),
        name="branch_merge",
    )(o, hy, wa, wr, gates, gates)


def _pack_halves(x):
    bits = pltpu.bitcast(x.astype(F32), jnp.uint32)
    half = x.shape[1] // 2
    return (bits[:, :half] >> 16) | bits[:, half:]


def _unpack_halves(w):
    lo = pltpu.bitcast(w << 16, F32).astype(BF16)
    hi = pltpu.bitcast(w & jnp.uint32(0xFFFF0000), F32).astype(BF16)
    return jnp.concatenate([lo, hi], axis=1)


def _outproj_kernel(m_ref, wo_ref, x_ref, g_ref, rw_ref, rb_ref, x1_ref, hf_ref, lg_ref):
    x1 = x_ref[...] + jnp.dot(m_ref[...], wo_ref[...], preferred_element_type=F32)
    x1_ref[...] = x1
    ms = jnp.mean(x1 * x1, axis=-1, keepdims=True)
    hf = (x1 * lax.rsqrt(ms + RMS_EPS) * g_ref[...]).astype(BF16)
    hf_ref[...] = _pack_halves(hf)
    lg_ref[...] = jnp.dot(hf, rw_ref[...], preferred_element_type=F32) + rb_ref[...]


def _outproj(merged, wo, x, g, rw, rb, tm=512):
    t, d = x.shape
    ne = rw.shape[1]
    rw_p = jnp.zeros((d, LANES), BF16).at[:, :ne].set(rw.astype(BF16))
    rb_p = jnp.zeros((1, LANES), F32).at[0, :ne].set(rb.astype(F32))
    row = lambda i: (i, 0)
    fixed = lambda i: (0, 0)
    return pl.pallas_call(
        _outproj_kernel,
        out_shape=(jax.ShapeDtypeStruct((t, d), F32),
                   jax.ShapeDtypeStruct((t, d // 2), jnp.uint32),
                   jax.ShapeDtypeStruct((t, LANES), F32)),
        grid=(t // tm,),
        in_specs=[pl.BlockSpec((tm, d), row),
                  pl.BlockSpec((d, d), fixed),
                  pl.BlockSpec((tm, d), row),
                  pl.BlockSpec((1, d), fixed),
                  pl.BlockSpec((d, LANES), fixed),
                  pl.BlockSpec((1, LANES), fixed)],
        out_specs=(pl.BlockSpec((tm, d), row),
                   pl.BlockSpec((tm, d // 2), row),
                   pl.BlockSpec((tm, LANES), row)),
        compiler_params=_cparams(("parallel",)),
        name="outproj_norm_router",
    )(merged, wo, x, g.reshape(1, d).astype(F32), rw_p, rb_p)


def _expert_changed(be_ref, r):
    return jnp.logical_or(r == 0, be_ref[r] != be_ref[jnp.maximum(r - 1, 0)])


def _expert_up_kernel(be_ref, nu_ref, xs_ref, wg_ref, bg_ref, wu_ref, bu_ref, act_ref,
                      wgb_ref, wub_ref):
    r = pl.program_id(1)
    used = r < nu_ref[0]

    @pl.when(jnp.logical_and(used, _expert_changed(be_ref, r)))
    def _():
        wgb_ref[...] = wg_ref[...].astype(BF16)
        wub_ref[...] = wu_ref[...].astype(BF16)

    @pl.when(used)
    def _():
        xs = _unpack_halves(xs_ref[...])
        glu = jnp.minimum(jnp.dot(xs, wgb_ref[...], preferred_element_type=F32) + bg_ref[...],
                          SWIGLU_LIMIT)
        lin = jnp.clip(jnp.dot(xs, wub_ref[...], preferred_element_type=F32) + bu_ref[...],
                       -SWIGLU_LIMIT, SWIGLU_LIMIT)
        act_ref[...] = (glu * _sigmoid(SWIGLU_ALPHA * glu) * (lin + 1.0)).astype(act_ref.dtype)

    @pl.when(jnp.logical_not(used))
    def _():
        act_ref[...] = jnp.zeros_like(act_ref)


def _expert_down_kernel(be_ref, nu_ref, act_ref, wd_ref, bd_ref, y_ref, wdb_ref):
    r = pl.program_id(1)
    used = r < nu_ref[0]

    @pl.when(jnp.logical_and(used, _expert_changed(be_ref, r)))
    def _():
        wdb_ref[...] = wd_ref[...].astype(BF16)

    @pl.when(used)
    def _():
        y_ref[...] = jnp.dot(act_ref[...], wdb_ref[...], preferred_element_type=F32) + bd_ref[...]

    @pl.when(jnp.logical_not(used))
    def _():
        y_ref[...] = jnp.zeros_like(y_ref)


def _experts(xs, block_expert, n_used, wg, bg, wu, bu, wd, bd, tf=1024, tn=1024):
    p = xs.shape[0]
    ne, d, dff = wg.shape
    nblk = p // MOE_BLOCK

    def rr(r, nu):
        return jnp.minimum(r, nu[0] - 1)

    wspec = lambda k, t: pl.BlockSpec((None, k, t), lambda c, r, be, nu: (be[rr(r, nu)], 0, c))
    bspec = lambda t: pl.BlockSpec((None, 1, t), lambda c, r, be, nu: (be[rr(r, nu)], 0, c))
    rows = lambda k: pl.BlockSpec((MOE_BLOCK, k), lambda c, r, be, nu: (rr(r, nu), 0))

    act = pl.pallas_call(
        _expert_up_kernel,
        out_shape=jax.ShapeDtypeStruct((p, dff), BF16),
        grid_spec=pltpu.PrefetchScalarGridSpec(
            num_scalar_prefetch=2,
            grid=(dff // tf, nblk),
            in_specs=[rows(d // 2), wspec(d, tf), bspec(tf), wspec(d, tf), bspec(tf)],
            out_specs=pl.BlockSpec((MOE_BLOCK, tf), lambda c, r, be, nu: (r, c)),
            scratch_shapes=[pltpu.VMEM((d, tf), BF16), pltpu.VMEM((d, tf), BF16)]),
        compiler_params=_cparams(("arbitrary", "arbitrary")),
        name="moe_up",
    )(block_expert, n_used, xs, wg, bg.reshape(ne, 1, dff).astype(F32), wu,
      bu.reshape(ne, 1, dff).astype(F32))

    return pl.pallas_call(
        _expert_down_kernel,
        out_shape=jax.ShapeDtypeStruct((p, d), F32),
        grid_spec=pltpu.PrefetchScalarGridSpec(
            num_scalar_prefetch=2,
            grid=(d // tn, nblk),
            in_specs=[rows(dff), wspec(dff, tn), bspec(tn)],
            out_specs=pl.BlockSpec((MOE_BLOCK, tn), lambda c, r, be, nu: (r, c)),
            scratch_shapes=[pltpu.VMEM((dff, tn), BF16)]),
        compiler_params=_cparams(("arbitrary", "arbitrary")),
        name="moe_down",
    )(block_expert, n_used, act, wd, bd.reshape(ne, 1, d).astype(F32))


def _combine_kernel(dest_ref, x1_ref, gw_ref, g_ref, y_hbm, o_ref, buf_ref, sem_ref,
                    *, tt, ct, norm):
    base = pl.program_id(0) * (tt * MOE_TOPK)
    nch = tt // ct

    def row_copy(row, slot, k, tok):
        return pltpu.make_async_copy(y_hbm.at[pl.ds(row, 1), :],
                                     buf_ref.at[slot, k, pl.ds(tok, 1), :], sem_ref.at[slot])

    def issue(c):
        def body(tok, _):
            for k in range(MOE_TOPK):
                row = dest_ref[base + (c * ct + tok) * MOE_TOPK + k]
                row_copy(row, c % 2, k, tok).start()
            return 0
        lax.fori_loop(0, ct, body, 0, unroll=2)

    def wait(c):
        for k in range(MOE_TOPK):
            pltpu.make_async_copy(y_hbm.at[pl.ds(0, ct), :], buf_ref.at[c % 2, k],
                                  sem_ref.at[c % 2]).wait()

    issue(0)
    for c in range(nch):
        if c + 1 < nch:
            issue(c + 1)
        wait(c)
        rows = pl.ds(c * ct, ct)
        acc = x1_ref[rows, :]
        gw = gw_ref[rows, :]
        for k in range(MOE_TOPK):
            acc = acc + gw[:, k:k + 1] * buf_ref[c % 2, k]
        if norm:
            ms = jnp.mean(acc * acc, axis=-1, keepdims=True)
            acc = acc * lax.rsqrt(ms + RMS_EPS) * g_ref[...]
        o_ref[rows, :] = acc


def _combine(x1, y, dest, gate_w, g, norm, tt=256, ct=64):
    t, d = x1.shape
    kern = functools.partial(_combine_kernel, tt=tt, ct=ct, norm=norm)
    return pl.pallas_call(
        kern,
        out_shape=jax.ShapeDtypeStruct((t, d), F32),
        grid_spec=pltpu.PrefetchScalarGridSpec(
            num_scalar_prefetch=1,
            grid=(t // tt,),
            in_specs=[pl.BlockSpec((tt, d), lambda i, dref: (i, 0)),
                      pl.BlockSpec((tt, MOE_TOPK), lambda i, dref: (i, 0)),
                      pl.BlockSpec((1, d), lambda i, dref: (0, 0)),
                      pl.BlockSpec(memory_space=pl.ANY)],
            out_specs=pl.BlockSpec((tt, d), lambda i, dref: (i, 0)),
            scratch_shapes=[pltpu.VMEM((2, MOE_TOPK, ct, d), F32),
                            pltpu.SemaphoreType.DMA((2,))]),
        compiler_params=_cparams(("arbitrary",)),
        name="moe_combine",
    )(dest.reshape(t * MOE_TOPK), x1, gate_w, g.reshape(1, d).astype(F32), y)


def _route_kernel(lg_ref, out_ref, cnt_ref, base_ref, *, n_experts, tm):
    @pl.when(pl.program_id(0) == 0)
    def _():
        base_ref[...] = jnp.zeros_like(base_ref)

    lane = lax.broadcasted_iota(jnp.int32, (tm, LANES), 1)
    lg = jnp.where(lane < n_experts, lg_ref[...], -jnp.inf)
    vals, ids = [], []
    onehot = jnp.zeros((tm, LANES), jnp.bool_)
    for _k in range(MOE_TOPK):
        mx = jnp.max(lg, axis=1, keepdims=True)
        first = jnp.min(jnp.where(lg == mx, lane, LANES), axis=1, keepdims=True)
        pick = lane == first
        onehot = jnp.logical_or(onehot, pick)
        lg = jnp.where(pick, -jnp.inf, lg)
        vals.append(mx)
        ids.append(first)
    exps = [jnp.exp(v - vals[0]) for v in vals]
    denom = exps[0] + exps[1] + exps[2] + exps[3]

    oh = jnp.where(onehot, 1.0, 0.0)
    r_i = lax.broadcasted_iota(jnp.int32, (tm, tm), 0)
    c_i = lax.broadcasted_iota(jnp.int32, (tm, tm), 1)
    earlier = jnp.where(r_i > c_i, 1.0, 0.0).astype(BF16)
    before = base_ref[...] + jnp.dot(earlier, oh.astype(BF16), preferred_element_type=F32)
    base_ref[...] = base_ref[...] + jnp.sum(oh, axis=0, keepdims=True)
    cnt_ref[...] = base_ref[...]

    out = jnp.zeros((tm, LANES), F32)
    for k in range(MOE_TOPK):
        rank = jnp.sum(jnp.where(lane == ids[k], before, 0.0), axis=1, keepdims=True)
        out = jnp.where(lane == k, exps[k] / denom, out)
        out = jnp.where(lane == MOE_TOPK + k, ids[k].astype(F32), out)
        out = jnp.where(lane == 2 * MOE_TOPK + k, rank, out)
    out_ref[...] = out


def _route(logits, n_experts, tm=512):
    t = logits.shape[0]
    kern = functools.partial(_route_kernel, n_experts=n_experts, tm=tm)
    packed, counts = pl.pallas_call(
        kern,
        out_shape=(jax.ShapeDtypeStruct((t, LANES), F32), jax.ShapeDtypeStruct((1, LANES), F32)),
        grid=(t // tm,),
        in_specs=[pl.BlockSpec((tm, LANES), lambda i: (i, 0))],
        out_specs=(pl.BlockSpec((tm, LANES), lambda i: (i, 0)),
                   pl.BlockSpec((1, LANES), lambda i: (0, 0))),
        scratch_shapes=[pltpu.VMEM((1, LANES), F32)],
        compiler_params=_cparams(("arbitrary",)),
        name="moe_route",
    )(logits)
    blk = MOE_BLOCK
    n_asg = t * MOE_TOPK
    p = (-(-n_asg // blk)) * blk + n_experts * blk
    cnt = counts[0, :n_experts].astype(jnp.int32)
    padded = (cnt + blk - 1) // blk * blk
    pad_end = jnp.cumsum(padded)
    pad_start = pad_end - padded
    row_starts = jnp.arange(p // blk, dtype=jnp.int32) * blk
    block_expert = jnp.minimum(jnp.sum(pad_end[None, :] <= row_starts[:, None], axis=1),
                               n_experts - 1).astype(jnp.int32)
    n_used = (pad_end[-1] // blk).astype(jnp.int32).reshape(1)
    gate_w = packed[:, :MOE_TOPK]
    ids = packed[:, MOE_TOPK:2 * MOE_TOPK].astype(jnp.int32)
    rank = packed[:, 2 * MOE_TOPK:3 * MOE_TOPK].astype(jnp.int32)
    sel = ids[:, :, None] == jnp.arange(n_experts, dtype=jnp.int32)[None, None, :]
    dest = rank + jnp.sum(jnp.where(sel, pad_start[None, None, :], 0), axis=-1)
    fill_start = jnp.minimum((pad_start + cnt) // SUBLANES * SUBLANES, p - blk).astype(jnp.int32)
    return gate_w, dest, block_expert, n_used, fill_start, p


def _dispatch_kernel(dest_ref, fill_ref, nu_ref, hf_ref, xs_hbm, zero_ref, sem_ref,
                     *, tt, n_experts, nblk):
    i = pl.program_id(0)

    def zero_fill(start):
        return pltpu.make_async_copy(zero_ref, xs_hbm.at[pl.ds(start, MOE_BLOCK), :],
                                     sem_ref.at[0])

    @pl.when(i == 0)
    def _():
        zero_ref[...] = jnp.zeros_like(zero_ref)
        for e in range(n_experts):
            fill = zero_fill(pl.multiple_of(fill_ref[e], SUBLANES))
            fill.start()
            fill.wait()

        def fill_block(r, _):
            fill = zero_fill(pl.multiple_of(r * MOE_BLOCK, MOE_BLOCK))
            fill.start()
            fill.wait()
            return 0

        lax.fori_loop(nu_ref[0], nblk, fill_block, 0)

    base = i * tt

    def body(tok, _):
        for k in range(MOE_TOPK):
            row = dest_ref[(base + tok) * MOE_TOPK + k]
            pltpu.make_async_copy(hf_ref.at[pl.ds(tok, 1), :],
                                  xs_hbm.at[pl.ds(row, 1), :], sem_ref.at[1]).start()
        return 0

    lax.fori_loop(0, tt, body, 0, unroll=2)
    for k in range(MOE_TOPK):
        pltpu.make_async_copy(hf_ref, xs_hbm.at[pl.ds(0, tt), :], sem_ref.at[1]).wait()


def _dispatch(hfp, dest, fill_start, n_used, p, n_experts, tt=512):
    t, w = hfp.shape
    kern = functools.partial(_dispatch_kernel, tt=tt, n_experts=n_experts, nblk=p // MOE_BLOCK)
    return pl.pallas_call(
        kern,
        out_shape=jax.ShapeDtypeStruct((p, w), hfp.dtype),
        grid_spec=pltpu.PrefetchScalarGridSpec(
            num_scalar_prefetch=3,
            grid=(t // tt,),
            in_specs=[pl.BlockSpec((tt, w), lambda i, dref, fref, nref: (i, 0))],
            out_specs=pl.BlockSpec(memory_space=pl.ANY),
            scratch_shapes=[pltpu.VMEM((MOE_BLOCK, w), hfp.dtype),
                            pltpu.SemaphoreType.DMA((2,))]),
        compiler_params=_cparams(("arbitrary",)),
        name="moe_dispatch",
    )(dest.reshape(t * MOE_TOPK), fill_start, n_used, hfp)


def _layer(x2, bsz, seq, norm_mix_g, w_in, conv_w, conv_b, lru_wa, lru_ba, lru_wx, lru_bx,
           lru_lambda, w_attn_branch, w_rnn_branch, w_out, norm_ffn_g, router_w, router_b,
           w_gate, b_gate, w_up, b_up, w_down, b_down):
    t, d = x2.shape
    aw = ATTN_HEADS * HEAD_DIM
    r = lru_lambda.shape[0]
    ne = router_w.shape[1]

    hm = _rmsnorm(x2, norm_mix_g, BF16)
    w_in_b = w_in.astype(BF16)
    qkv = _proj(hm, w_in_b, 0, 3 * aw, BF16, "scale_head",
                n_scaled_tiles=aw // 1024, scale=HEAD_DIM ** -0.5 * LOG2E)
    xr = _proj(hm, w_in_b, 3 * aw, r, F32)
    yr = _proj(hm, w_in_b, 3 * aw + r, r, F32, "gelu")
    gates = _proj(hm, w_in_b, 3 * aw + 2 * r, 2 * d, F32, "sigmoid")

    o = _moba(qkv.reshape(bsz, seq, 3 * aw), bsz, seq).reshape(t, aw)
    hy = _lru(xr.reshape(bsz, seq, r), yr.reshape(bsz, seq, r), conv_w, conv_b,
              lru_wa, lru_ba, lru_wx, lru_bx, lru_lambda, bsz, seq).reshape(t, r)

    merged = _merge(o, hy, w_attn_branch.astype(BF16), w_rnn_branch.astype(BF16), gates)
    x1, hf, logits = _outproj(merged, w_out.astype(BF16), x2, norm_ffn_g, router_w, router_b)

    gate_w, dest, block_expert, n_used, fill_start, p = _route(logits, ne)
    xs = _dispatch(hf, dest, fill_start, n_used, p, ne)
    y = _experts(xs, block_expert, n_used, w_gate, b_gate, w_up, b_up, w_down, b_down)
    return x1, y, dest, gate_w


def kernel(x, norm_mix_g, w_in, conv_w, conv_b, lru_wa, lru_ba, lru_wx, lru_bx, lru_lambda, w_attn_branch, w_rnn_branch, w_out, norm_ffn_g, router_w, router_b, w_gate, b_gate, w_up, b_up, w_down, b_down, norm_final_g):
    bsz, seq, d = x.shape
    depth = w_in.shape[0]
    assert seq % (2 * MOBA_BLOCK) == 0 and d % LANES == 0
    x2 = x.reshape(bsz * seq, d)
    for l in range(depth):
        x1, y, dest, gate_w = _layer(
            x2, bsz, seq, norm_mix_g[l], w_in[l], conv_w[l], conv_b[l], lru_wa[l], lru_ba[l],
            lru_wx[l], lru_bx[l], lru_lambda[l], w_attn_branch[l], w_rnn_branch[l], w_out[l],
            norm_ffn_g[l], router_w[l], router_b[l], w_gate[l], b_gate[l], w_up[l], b_up[l],
            w_down[l], b_down[l])
        last = l + 1 == depth
        x2 = _combine(x1, y, dest, gate_w, norm_final_g, norm=last)
    return x2.reshape(bsz, seq, d)
```

```python
import functools

import jax
import jax.numpy as jnp
from jax import lax
from jax.experimental import pallas as pl
from jax.experimental.pallas import tpu as pltpu

ATTN_HEADS = 16
HEAD_DIM = 128
MOBA_BLOCK = 256
MOBA_TOPK = 3
LRU_BLOCKS = 8
CONV_WIDTH = 4
LRU_C = 8.0
MOE_TOPK = 4
SWIGLU_ALPHA = 1.702
SWIGLU_LIMIT = 7.0
MOE_BLOCK = 512
RMS_EPS = 1e-6

LANES = 128
SUBLANES = 8
VMEM_LIMIT = 56 * 1024 * 1024
LOG2E = 1.4426950408889634
NEG_BIG = -1e30

BF16 = jnp.bfloat16
F32 = jnp.float32


def _cparams(sem):
    return pltpu.CompilerParams(dimension_semantics=sem, vmem_limit_bytes=VMEM_LIMIT)


def _sigmoid(x):
    return 1.0 / (1.0 + jnp.exp(-x))


def _gelu_tanh(x):
    c = 0.7978845608028654
    return 0.5 * x * (1.0 + jnp.tanh(c * (x + 0.044715 * (x * x * x))))


def _rmsnorm_kernel(x_ref, g_ref, o_ref):
    x = x_ref[...]
    ms = jnp.mean(x * x, axis=-1, keepdims=True)
    o_ref[...] = (x * lax.rsqrt(ms + RMS_EPS) * g_ref[...]).astype(o_ref.dtype)


def _rmsnorm(x, g, out_dtype, tm=1024):
    t, d = x.shape
    return pl.pallas_call(
        _rmsnorm_kernel,
        out_shape=jax.ShapeDtypeStruct((t, d), out_dtype),
        grid=(t // tm,),
        in_specs=[pl.BlockSpec((tm, d), lambda i: (i, 0)),
                  pl.BlockSpec((1, d), lambda i: (0, 0))],
        out_specs=pl.BlockSpec((tm, d), lambda i: (i, 0)),
        compiler_params=_cparams(("parallel",)),
        name="rmsnorm",
    )(x, g.reshape(1, d).astype(F32))


def _proj_kernel(a_ref, w_ref, o_ref, *, epilogue, n_scaled_tiles, scale):
    acc = jnp.dot(a_ref[...], w_ref[...], preferred_element_type=F32)
    if epilogue == "scale_head":
        s = jnp.where(pl.program_id(1) < n_scaled_tiles, scale, 1.0).astype(F32)
        acc = acc * s
    elif epilogue == "gelu":
        acc = _gelu_tanh(acc)
    elif epilogue == "sigmoid":
        acc = _sigmoid(acc)
    o_ref[...] = acc.astype(o_ref.dtype)


def _proj(a, w, col0, n, out_dtype, epilogue="none", n_scaled_tiles=0, scale=1.0,
          tm=1024, tn=1024):
    t, k = a.shape
    c0 = col0 // tn
    kern = functools.partial(_proj_kernel, epilogue=epilogue,
                             n_scaled_tiles=n_scaled_tiles, scale=scale)
    return pl.pallas_call(
        kern,
        out_shape=jax.ShapeDtypeStruct((t, n), out_dtype),
        grid=(t // tm, n // tn),
        in_specs=[pl.BlockSpec((tm, k), lambda i, j: (i, 0)),
                  pl.BlockSpec((k, tn), lambda i, j: (0, c0 + j))],
        out_specs=pl.BlockSpec((tm, tn), lambda i, j: (i, j)),
        compiler_params=_cparams(("parallel", "parallel")),
        name="proj_" + epilogue,
    )(a, w)


def _moba_kernel(q_ref, k_ref, v_ref, o_ref, kext_ref, vt_ref, s_ref, *, seq):
    blk = MOBA_BLOCK
    qt = 2 * blk
    nsb = seq // qt

    kext_ref[:, :HEAD_DIM] = k_ref[...]
    r_blk = lax.broadcasted_iota(jnp.int32, (seq, LANES), 0) // blk
    c_id = lax.broadcasted_iota(jnp.int32, (seq, LANES), 1)
    kext_ref[:, HEAD_DIM:] = jnp.where(r_blk == c_id, 1.0, 0.0).astype(BF16)
    vt_ref[...] = v_ref[...].T

    a_row = lax.broadcasted_iota(jnp.int32, (LANES, seq), 0)
    a_col = lax.broadcasted_iota(jnp.int32, (LANES, seq), 1) // blk
    avg = jnp.where(a_row == a_col, 1.0 / blk, 0.0).astype(BF16)
    kmean = jnp.dot(avg, k_ref[...], preferred_element_type=F32).astype(BF16)

    nrow = 32
    brow = lax.broadcasted_iota(jnp.int32, (nrow, qt), 0)
    second = lax.broadcasted_iota(jnp.int32, (nrow, qt), 1) >= blk
    key_i = lax.broadcasted_iota(jnp.int32, (qt, qt), 0)
    qry_i = lax.broadcasted_iota(jnp.int32, (qt, qt), 1)
    same_blk = (key_i >= blk) == (qry_i >= blk)
    causal_self = jnp.logical_and(same_blk, key_i <= qry_i)
    cross = jnp.logical_and(key_i < blk, qry_i >= blk)

    def q_super(a, _):
        row0 = pl.multiple_of(a * qt, qt)
        q = q_ref[pl.ds(row0, qt), :]
        q_t = q.T
        gate = jnp.dot(kmean, q_t, preferred_element_type=F32)[:nrow, :]
        qblk = 2 * a + second.astype(jnp.int32)
        past = brow < qblk
        g = jnp.where(past, gate, -jnp.inf)
        sel = jnp.zeros((nrow, qt), jnp.bool_)
        for _k in range(MOBA_TOPK):
            mx = jnp.max(g, axis=0, keepdims=True)
            first = jnp.min(jnp.where(g == mx, brow, nrow), axis=0, keepdims=True)
            pick = brow == first
            sel = jnp.logical_or(sel, pick)
            g = jnp.where(pick, -jnp.inf, g)
        chosen = jnp.logical_and(sel, past)
        bias_t = jnp.where(chosen, 0.0, NEG_BIG)
        bias_t = jnp.concatenate([bias_t, jnp.zeros((LANES - nrow, qt), F32)], axis=0)
        qext_t = jnp.concatenate([q_t, bias_t.astype(BF16)], axis=0)

        prev_sel = jnp.max(jnp.where(jnp.logical_and(chosen, brow == 2 * a), 1.0, 0.0),
                           axis=0, keepdims=True) > 0.5
        allowed = jnp.logical_or(causal_self, jnp.logical_and(cross, prev_sel))
        s_diag = jnp.where(allowed, jnp.dot(k_ref[pl.ds(row0, qt), :], q_t,
                                            preferred_element_type=F32), NEG_BIG)
        s_ref[0] = s_diag
        mx_diag = jnp.max(s_diag, axis=0, keepdims=True)

        def absorb(m, l, acc, slot, mx, blk_idx):
            c0 = pl.multiple_of(blk_idx * qt, qt)
            m_new = jnp.maximum(m, mx)
            alpha = jnp.exp2(m - m_new)
            p = jnp.exp2(s_ref[slot] - m_new)
            l = alpha * l + jnp.sum(p, axis=0, keepdims=True)
            acc = alpha * acc + jnp.dot(vt_ref[:, pl.ds(c0, qt)], p.astype(BF16),
                                        preferred_element_type=F32)
            return m_new, l, acc

        def kv_step(carry, b, slot):
            m, l, acc, mx_cur, blk_cur = carry
            c_next = pl.multiple_of(b * qt, qt)
            s_next = jnp.dot(kext_ref[pl.ds(c_next, qt), :], qext_t,
                             preferred_element_type=F32)
            s_ref[1 - slot] = s_next
            mx_next = jnp.max(s_next, axis=0, keepdims=True)
            m, l, acc = absorb(m, l, acc, slot, mx_cur, blk_cur)
            return m, l, acc, mx_next, b

        def kv_pair(b0, carry):
            return kv_step(kv_step(carry, b0, 0), b0 + 1, 1)

        def kv_quad(t, carry):
            return kv_pair(4 * t + 2, kv_pair(4 * t, carry))

        init = (jnp.full((1, qt), NEG_BIG, F32), jnp.zeros((1, qt), F32),
                jnp.zeros((HEAD_DIM, qt), F32), mx_diag, a)
        carry = lax.fori_loop(0, a // 4, kv_quad, init)
        carry = lax.cond(a % 4 >= 2, lambda c: kv_pair(a // 4 * 4, c), lambda c: c, carry)

        def odd_tail(carry):
            m, l, acc, mx_last, blk_last = kv_step(carry, a - 1, 0)
            return absorb(m, l, acc, 1, mx_last, blk_last)

        def even_tail(carry):
            m, l, acc, mx_last, blk_last = carry
            return absorb(m, l, acc, 0, mx_last, blk_last)

        m, l, acc = lax.cond(a % 2 == 1, odd_tail, even_tail, carry)
        o_ref[pl.ds(row0, qt), :] = (acc * (1.0 / l)).T.astype(o_ref.dtype)
        return 0

    lax.fori_loop(0, nsb, q_super, 0)


def _moba(qkv, bsz, seq):
    h = ATTN_HEADS
    kern = functools.partial(_moba_kernel, seq=seq)
    return pl.pallas_call(
        kern,
        out_shape=jax.ShapeDtypeStruct((bsz, seq, h * HEAD_DIM), BF16),
        grid=(bsz, h),
        in_specs=[pl.BlockSpec((None, seq, HEAD_DIM), lambda b, hh: (b, 0, hh)),
                  pl.BlockSpec((None, seq, HEAD_DIM), lambda b, hh: (b, 0, h + hh)),
                  pl.BlockSpec((None, seq, HEAD_DIM), lambda b, hh: (b, 0, 2 * h + hh))],
        out_specs=pl.BlockSpec((None, seq, HEAD_DIM), lambda b, hh: (b, 0, hh)),
        scratch_shapes=[pltpu.VMEM((seq, 2 * HEAD_DIM), BF16),
                        pltpu.VMEM((HEAD_DIM, seq), BF16),
                        pltpu.VMEM((2, 2 * MOBA_BLOCK, 2 * MOBA_BLOCK), F32)],
        compiler_params=_cparams(("parallel", "parallel")),
        name="moba_attention",
    )(qkv, qkv, qkv)


def _lru_kernel(x_ref, y_ref, cw_ref, cb_ref, w_ref, bias_ref, lam_ref, o_ref,
                win_ref, a_ref, b_ref, h_ref, *, ts):
    t_idx = pl.program_id(2)

    @pl.when(t_idx == 0)
    def _():
        win_ref[0:SUBLANES, :] = jnp.zeros((SUBLANES, win_ref.shape[1]), F32)
        h_ref[...] = jnp.zeros_like(h_ref)

    win_ref[SUBLANES:, :] = x_ref[...]
    xc = cb_ref[...] + cw_ref[CONV_WIDTH - 1:CONV_WIDTH, :] * x_ref[...]
    for k in range(CONV_WIDTH - 1):
        shift = CONV_WIDTH - 1 - k
        xc = xc + cw_ref[k:k + 1, :] * win_ref[SUBLANES - shift:SUBLANES - shift + ts, :]
    win_ref[0:SUBLANES, :] = win_ref[ts:ts + SUBLANES, :]

    gates = jnp.dot(xc.astype(BF16), w_ref[...], preferred_element_type=F32) + bias_ref[...]
    cblk = xc.shape[1]
    rt = _sigmoid(gates[:, :cblk])
    it = _sigmoid(gates[:, cblk:])
    neg_lam = -lam_ref[...]
    softplus = jnp.maximum(neg_lam, 0.0) + jnp.log1p(jnp.exp(-jnp.abs(neg_lam)))
    log_a = (-LRU_C) * rt * softplus
    a = jnp.exp(log_a)
    a_ref[...] = a
    b_ref[...] = jnp.sqrt(-jnp.tanh(log_a) * (a * a + 1.0)) * (it * xc)

    row = lax.broadcasted_iota(jnp.int32, (SUBLANES, cblk), 0)

    def group(gi, h_in):
        r0 = pl.multiple_of(gi * SUBLANES, SUBLANES)
        a = a_ref[pl.ds(r0, SUBLANES), :]
        b = b_ref[pl.ds(r0, SUBLANES), :]
        for d in (1, 2, 4):
            keep = row >= d
            a_sh = jnp.where(keep, pltpu.roll(a, d, axis=0), 1.0)
            b_sh = jnp.where(keep, pltpu.roll(b, d, axis=0), 0.0)
            b = a * b_sh + b
            a = a * a_sh
        hs = b + a * h_in
        o_ref[pl.ds(r0, SUBLANES), :] = (hs * y_ref[pl.ds(r0, SUBLANES), :]).astype(o_ref.dtype)
        return jnp.broadcast_to(hs[SUBLANES - 1:SUBLANES, :], (SUBLANES, cblk))

    h_ref[...] = lax.fori_loop(0, ts // SUBLANES, group, h_ref[...], unroll=4)


def _lru(xr, yr, conv_w, conv_b, wa, ba, wx, bx, lam, bsz, seq, ts=1024):
    r = xr.shape[-1]
    nblk = LRU_BLOCKS
    cblk = r // nblk
    w_cat = jnp.concatenate([wa, wx], axis=-1).astype(BF16)
    b_cat = jnp.concatenate([ba.reshape(nblk, 1, cblk), bx.reshape(nblk, 1, cblk)], axis=-1).astype(F32)
    kern = functools.partial(_lru_kernel, ts=ts)
    chan = lambda b, n, t: (0, n)
    return pl.pallas_call(
        kern,
        out_shape=jax.ShapeDtypeStruct((bsz, seq, r), BF16),
        grid=(bsz, nblk, seq // ts),
        in_specs=[pl.BlockSpec((None, ts, cblk), lambda b, n, t: (b, t, n)),
                  pl.BlockSpec((None, ts, cblk), lambda b, n, t: (b, t, n)),
                  pl.BlockSpec((CONV_WIDTH, cblk), chan),
                  pl.BlockSpec((1, cblk), chan),
                  pl.BlockSpec((None, cblk, 2 * cblk), lambda b, n, t: (n, 0, 0)),
                  pl.BlockSpec((None, 1, 2 * cblk), lambda b, n, t: (n, 0, 0)),
                  pl.BlockSpec((1, cblk), chan)],
        out_specs=pl.BlockSpec((None, ts, cblk), lambda b, n, t: (b, t, n)),
        scratch_shapes=[pltpu.VMEM((ts + SUBLANES, cblk), F32),
                        pltpu.VMEM((ts, cblk), F32),
                        pltpu.VMEM((ts, cblk), F32),
                        pltpu.VMEM((SUBLANES, cblk), F32)],
        compiler_params=_cparams(("parallel", "parallel", "arbitrary")),
        name="conv_rglru",
    )(xr, yr, conv_w.astype(F32), conv_b.reshape(1, r).astype(F32), w_cat, b_cat,
      lam.reshape(1, r).astype(F32))


def _merge_kernel(o_ref, hy_ref, wa_ref, wr_ref, ga_ref, gr_ref, out_ref):
    ya = jnp.dot(o_ref[...], wa_ref[...], preferred_element_type=F32)
    yr = jnp.dot(hy_ref[...], wr_ref[...], preferred_element_type=F32)
    out_ref[...] = (ga_ref[...] * ya + gr_ref[...] * yr).astype(out_ref.dtype)


def _merge(o, hy, wa, wr, gates, tm=1024, tn=512):
    t, k = o.shape
    kr = hy.shape[1]
    n = wa.shape[1]
    nj = n // tn
    return pl.pallas_call(
        _merge_kernel,
        out_shape=jax.ShapeDtypeStruct((t, n), BF16),
        grid=(t // tm, nj),
        in_specs=[pl.BlockSpec((tm, k), lambda i, j: (i, 0)),
                  pl.BlockSpec((tm, kr), lambda i, j: (i, 0)),
                  pl.BlockSpec((k, tn), lambda i, j: (0, j)),
                  pl.BlockSpec((kr, tn), lambda i, j: (0, j)),
                  pl.BlockSpec((tm, tn), lambda i, j: (i, j)),
                  pl.BlockSpec((tm, tn), lambda i, j: (i, nj + j))],
        out_specs=pl.BlockSpec((tm, tn), lambda i, j: (i, j)),
        compiler_params=_cparams(("parallel", "parallel")),
        name="branch_merge",
    )(o, hy, wa, wr, gates, gates)


def _pack_halves(x):
    bits = pltpu.bitcast(x.astype(F32), jnp.uint32)
    half = x.shape[1] // 2
    return (bits[:, :half] >> 16) | bits[:, half:]


def _unpack_halves(w):
    lo = pltpu.bitcast(w << 16, F32).astype(BF16)
    hi = pltpu.bitcast(w & jnp.uint32(0xFFFF0000), F32).astype(BF16)
    return jnp.concatenate([lo, hi], axis=1)


def _outproj_kernel(m_ref, wo_ref, x_ref, g_ref, rw_ref, rb_ref, x1_ref, hf_ref, lg_ref):
    x1 = x_ref[...] + jnp.dot(m_ref[...], wo_ref[...], preferred_element_type=F32)
    x1_ref[...] = x1
    ms = jnp.mean(x1 * x1, axis=-1, keepdims=True)
    hf = (x1 * lax.rsqrt(ms + RMS_EPS) * g_ref[...]).astype(BF16)
    hf_ref[...] = _pack_halves(hf)
    lg_ref[...] = jnp.dot(hf, rw_ref[...], preferred_element_type=F32) + rb_ref[...]


def _outproj(merged, wo, x, g, rw, rb, tm=512):
    t, d = x.shape
    ne = rw.shape[1]
    rw_p = jnp.zeros((d, LANES), BF16).at[:, :ne].set(rw.astype(BF16))
    rb_p = jnp.zeros((1, LANES), F32).at[0, :ne].set(rb.astype(F32))
    row = lambda i: (i, 0)
    fixed = lambda i: (0, 0)
    return pl.pallas_call(
        _outproj_kernel,
        out_shape=(jax.ShapeDtypeStruct((t, d), F32),
                   jax.ShapeDtypeStruct((t, d // 2), jnp.uint32),
                   jax.ShapeDtypeStruct((t, LANES), F32)),
        grid=(t // tm,),
        in_specs=[pl.BlockSpec((tm, d), row),
                  pl.BlockSpec((d, d), fixed),
                  pl.BlockSpec((tm, d), row),
                  pl.BlockSpec((1, d), fixed),
                  pl.BlockSpec((d, LANES), fixed),
                  pl.BlockSpec((1, LANES), fixed)],
        out_specs=(pl.BlockSpec((tm, d), row),
                   pl.BlockSpec((tm, d // 2), row),
                   pl.BlockSpec((tm, LANES), row)),
        compiler_params=_cparams(("parallel",)),
        name="outproj_norm_router",
    )(merged, wo, x, g.reshape(1, d).astype(F32), rw_p, rb_p)


def _expert_changed(be_ref, r):
    return jnp.logical_or(r == 0, be_ref[r] != be_ref[jnp.maximum(r - 1, 0)])


def _expert_up_kernel(be_ref, nu_ref, xs_ref, wg_ref, bg_ref, wu_ref, bu_ref, act_ref,
                      wgb_ref, wub_ref):
    r = pl.program_id(1)
    used = r < nu_ref[0]

    @pl.when(jnp.logical_and(used, _expert_changed(be_ref, r)))
    def _():
        wgb_ref[...] = wg_ref[...].astype(BF16)
        wub_ref[...] = wu_ref[...].astype(BF16)

    @pl.when(used)
    def _():
        xs = _unpack_halves(xs_ref[...])
        glu = jnp.minimum(jnp.dot(xs, wgb_ref[...], preferred_element_type=F32) + bg_ref[...],
                          SWIGLU_LIMIT)
        lin = jnp.clip(jnp.dot(xs, wub_ref[...], preferred_element_type=F32) + bu_ref[...],
                       -SWIGLU_LIMIT, SWIGLU_LIMIT)
        act_ref[...] = (glu * _sigmoid(SWIGLU_ALPHA * glu) * (lin + 1.0)).astype(act_ref.dtype)

    @pl.when(jnp.logical_not(used))
    def _():
        act_ref[...] = jnp.zeros_like(act_ref)


def _expert_down_kernel(be_ref, nu_ref, act_ref, wd_ref, bd_ref, y_ref, wdb_ref):
    r = pl.program_id(1)
    used = r < nu_ref[0]

    @pl.when(jnp.logical_and(used, _expert_changed(be_ref, r)))
    def _():
        wdb_ref[...] = wd_ref[...].astype(BF16)

    @pl.when(used)
    def _():
        y_ref[...] = jnp.dot(act_ref[...], wdb_ref[...], preferred_element_type=F32) + bd_ref[...]

    @pl.when(jnp.logical_not(used))
    def _():
        y_ref[...] = jnp.zeros_like(y_ref)


def _experts(xs, block_expert, n_used, wg, bg, wu, bu, wd, bd, tf=1024, tn=1024):
    p = xs.shape[0]
    ne, d, dff = wg.shape
    nblk = p // MOE_BLOCK

    def rr(r, nu):
        return jnp.minimum(r, nu[0] - 1)

    wspec = lambda k, t: pl.BlockSpec((None, k, t), lambda c, r, be, nu: (be[rr(r, nu)], 0, c))
    bspec = lambda t: pl.BlockSpec((None, 1, t), lambda c, r, be, nu: (be[rr(r, nu)], 0, c))
    rows = lambda k: pl.BlockSpec((MOE_BLOCK, k), lambda c, r, be, nu: (rr(r, nu), 0))

    act = pl.pallas_call(
        _expert_up_kernel,
        out_shape=jax.ShapeDtypeStruct((p, dff), BF16),
        grid_spec=pltpu.PrefetchScalarGridSpec(
            num_scalar_prefetch=2,
            grid=(dff // tf, nblk),
            in_specs=[rows(d // 2), wspec(d, tf), bspec(tf), wspec(d, tf), bspec(tf)],
            out_specs=pl.BlockSpec((MOE_BLOCK, tf), lambda c, r, be, nu: (r, c)),
            scratch_shapes=[pltpu.VMEM((d, tf), BF16), pltpu.VMEM((d, tf), BF16)]),
        compiler_params=_cparams(("arbitrary", "arbitrary")),
        name="moe_up",
    )(block_expert, n_used, xs, wg, bg.reshape(ne, 1, dff).astype(F32), wu,
      bu.reshape(ne, 1, dff).astype(F32))

    return pl.pallas_call(
        _expert_down_kernel,
        out_shape=jax.ShapeDtypeStruct((p, d), F32),
        grid_spec=pltpu.PrefetchScalarGridSpec(
            num_scalar_prefetch=2,
            grid=(d // tn, nblk),
            in_specs=[rows(dff), wspec(dff, tn), bspec(tn)],
            out_specs=pl.BlockSpec((MOE_BLOCK, tn), lambda c, r, be, nu: (r, c)),
            scratch_shapes=[pltpu.VMEM((dff, tn), BF16)]),
        compiler_params=_cparams(("arbitrary", "arbitrary")),
        name="moe_down",
    )(block_expert, n_used, act, wd, bd.reshape(ne, 1, d).astype(F32))


def _combine_kernel(dest_ref, x1_ref, gw_ref, g_ref, y_hbm, o_ref, buf_ref, sem_ref,
                    *, tt, ct, norm):
    base = pl.program_id(0) * (tt * MOE_TOPK)
    nch = tt // ct

    def row_copy(row, slot, k, tok):
        return pltpu.make_async_copy(y_hbm.at[pl.ds(row, 1), :],
                                     buf_ref.at[slot, k, pl.ds(tok, 1), :], sem_ref.at[slot])

    def issue(c):
        def body(tok, _):
            for k in range(MOE_TOPK):
                row = dest_ref[base + (c * ct + tok) * MOE_TOPK + k]
                row_copy(row, c % 2, k, tok).start()
            return 0
        lax.fori_loop(0, ct, body, 0, unroll=2)

    def wait(c):
        for k in range(MOE_TOPK):
            pltpu.make_async_copy(y_hbm.at[pl.ds(0, ct), :], buf_ref.at[c % 2, k],
                                  sem_ref.at[c % 2]).wait()

    issue(0)
    for c in range(nch):
        if c + 1 < nch:
            issue(c + 1)
        wait(c)
        rows = pl.ds(c * ct, ct)
        acc = x1_ref[rows, :]
        gw = gw_ref[rows, :]
        for k in range(MOE_TOPK):
            acc = acc + gw[:, k:k + 1] * buf_ref[c % 2, k]
        if norm:
            ms = jnp.mean(acc * acc, axis=-1, keepdims=True)
            acc = acc * lax.rsqrt(ms + RMS_EPS) * g_ref[...]
        o_ref[rows, :] = acc


def _combine(x1, y, dest, gate_w, g, norm, tt=256, ct=64):
    t, d = x1.shape
    kern = functools.partial(_combine_kernel, tt=tt, ct=ct, norm=norm)
    return pl.pallas_call(
        kern,
        out_shape=jax.ShapeDtypeStruct((t, d), F32),
        grid_spec=pltpu.PrefetchScalarGridSpec(
            num_scalar_prefetch=1,
            grid=(t // tt,),
            in_specs=[pl.BlockSpec((tt, d), lambda i, dref: (i, 0)),
                      pl.BlockSpec((tt, MOE_TOPK), lambda i, dref: (i, 0)),
                      pl.BlockSpec((1, d), lambda i, dref: (0, 0)),
                      pl.BlockSpec(memory_space=pl.ANY)],
            out_specs=pl.BlockSpec((tt, d), lambda i, dref: (i, 0)),
            scratch_shapes=[pltpu.VMEM((2, MOE_TOPK, ct, d), F32),
                            pltpu.SemaphoreType.DMA((2,))]),
        compiler_params=_cparams(("arbitrary",)),
        name="moe_combine",
    )(dest.reshape(t * MOE_TOPK), x1, gate_w, g.reshape(1, d).astype(F32), y)


def _route_kernel(lg_ref, out_ref, cnt_ref, base_ref, *, n_experts, tm):
    @pl.when(pl.program_id(0) == 0)
    def _():
        base_ref[...] = jnp.zeros_like(base_ref)

    lane = lax.broadcasted_iota(jnp.int32, (tm, LANES), 1)
    lg = jnp.where(lane < n_experts, lg_ref[...], -jnp.inf)
    vals, ids = [], []
    onehot = jnp.zeros((tm, LANES), jnp.bool_)
    for _k in range(MOE_TOPK):
        mx = jnp.max(lg, axis=1, keepdims=True)
        first = jnp.min(jnp.where(lg == mx, lane, LANES), axis=1, keepdims=True)
        pick = lane == first
        onehot = jnp.logical_or(onehot, pick)
        lg = jnp.where(pick, -jnp.inf, lg)
        vals.append(mx)
        ids.append(first)
    exps = [jnp.exp(v - vals[0]) for v in vals]
    denom = exps[0] + exps[1] + exps[2] + exps[3]

    oh = jnp.where(onehot, 1.0, 0.0)
    r_i = lax.broadcasted_iota(jnp.int32, (tm, tm), 0)
    c_i = lax.broadcasted_iota(jnp.int32, (tm, tm), 1)
    earlier = jnp.where(r_i > c_i, 1.0, 0.0).astype(BF16)
    before = base_ref[...] + jnp.dot(earlier, oh.astype(BF16), preferred_element_type=F32)
    base_ref[...] = base_ref[...] + jnp.sum(oh, axis=0, keepdims=True)
    cnt_ref[...] = base_ref[...]

    out = jnp.zeros((tm, LANES), F32)
    for k in range(MOE_TOPK):
        rank = jnp.sum(jnp.where(lane == ids[k], before, 0.0), axis=1, keepdims=True)
        out = jnp.where(lane == k, exps[k] / denom, out)
        out = jnp.where(lane == MOE_TOPK + k, ids[k].astype(F32), out)
        out = jnp.where(lane == 2 * MOE_TOPK + k, rank, out)
    out_ref[...] = out


def _route(logits, n_experts, tm=512):
    t = logits.shape[0]
    kern = functools.partial(_route_kernel, n_experts=n_experts, tm=tm)
    packed, counts = pl.pallas_call(
        kern,
        out_shape=(jax.ShapeDtypeStruct((t, LANES), F32), jax.ShapeDtypeStruct((1, LANES), F32)),
        grid=(t // tm,),
        in_specs=[pl.BlockSpec((tm, LANES), lambda i: (i, 0))],
        out_specs=(pl.BlockSpec((tm, LANES), lambda i: (i, 0)),
                   pl.BlockSpec((1, LANES), lambda i: (0, 0))),
        scratch_shapes=[pltpu.VMEM((1, LANES), F32)],
        compiler_params=_cparams(("arbitrary",)),
        name="moe_route",
    )(logits)
    blk = MOE_BLOCK
    n_asg = t * MOE_TOPK
    p = (-(-n_asg // blk)) * blk + n_experts * blk
    cnt = counts[0, :n_experts].astype(jnp.int32)
    padded = (cnt + blk - 1) // blk * blk
    pad_end = jnp.cumsum(padded)
    pad_start = pad_end - padded
    row_starts = jnp.arange(p // blk, dtype=jnp.int32) * blk
    block_expert = jnp.minimum(jnp.sum(pad_end[None, :] <= row_starts[:, None], axis=1),
                               n_experts - 1).astype(jnp.int32)
    n_used = (pad_end[-1] // blk).astype(jnp.int32).reshape(1)
    gate_w = packed[:, :MOE_TOPK]
    ids = packed[:, MOE_TOPK:2 * MOE_TOPK].astype(jnp.int32)
    rank = packed[:, 2 * MOE_TOPK:3 * MOE_TOPK].astype(jnp.int32)
    sel = ids[:, :, None] == jnp.arange(n_experts, dtype=jnp.int32)[None, None, :]
    dest = rank + jnp.sum(jnp.where(sel, pad_start[None, None, :], 0), axis=-1)
    fill_start = jnp.minimum((pad_start + cnt) // SUBLANES * SUBLANES, p - blk).astype(jnp.int32)
    return gate_w, dest, block_expert, n_used, fill_start, p


def _dispatch_kernel(dest_ref, fill_ref, nu_ref, hf_ref, xs_hbm, zero_ref, sem_ref,
                     *, tt, n_experts, nblk):
    i = pl.program_id(0)

    def zero_fill(start):
        return pltpu.make_async_copy(zero_ref, xs_hbm.at[pl.ds(start, MOE_BLOCK), :],
                                     sem_ref.at[0])

    @pl.when(i == 0)
    def _():
        zero_ref[...] = jnp.zeros_like(zero_ref)
        for e in range(n_experts):
            fill = zero_fill(pl.multiple_of(fill_ref[e], SUBLANES))
            fill.start()
            fill.wait()

        def fill_block(r, _):
            fill = zero_fill(pl.multiple_of(r * MOE_BLOCK, MOE_BLOCK))
            fill.start()
            fill.wait()
            return 0

        lax.fori_loop(nu_ref[0], nblk, fill_block, 0)

    base = i * tt

    def body(tok, _):
        for k in range(MOE_TOPK):
            row = dest_ref[(base + tok) * MOE_TOPK + k]
            pltpu.make_async_copy(hf_ref.at[pl.ds(tok, 1), :],
                                  xs_hbm.at[pl.ds(row, 1), :], sem_ref.at[1]).start()
        return 0

    lax.fori_loop(0, tt, body, 0, unroll=2)
    for k in range(MOE_TOPK):
        pltpu.make_async_copy(hf_ref, xs_hbm.at[pl.ds(0, tt), :], sem_ref.at[1]).wait()


def _dispatch(hfp, dest, fill_start, n_used, p, n_experts, tt=512):
    t, w = hfp.shape
    kern = functools.partial(_dispatch_kernel, tt=tt, n_experts=n_experts, nblk=p // MOE_BLOCK)
    return pl.pallas_call(
        kern,
        out_shape=jax.ShapeDtypeStruct((p, w), hfp.dtype),
        grid_spec=pltpu.PrefetchScalarGridSpec(
            num_scalar_prefetch=3,
            grid=(t // tt,),
            in_specs=[pl.BlockSpec((tt, w), lambda i, dref, fref, nref: (i, 0))],
            out_specs=pl.BlockSpec(memory_space=pl.ANY),
            scratch_shapes=[pltpu.VMEM((MOE_BLOCK, w), hfp.dtype),
                            pltpu.SemaphoreType.DMA((2,))]),
        compiler_params=_cparams(("arbitrary",)),
        name="moe_dispatch",
    )(dest.reshape(t * MOE_TOPK), fill_start, n_used, hfp)


def _layer(x2, bsz, seq, norm_mix_g, w_in, conv_w, conv_b, lru_wa, lru_ba, lru_wx, lru_bx,
           lru_lambda, w_attn_branch, w_rnn_branch, w_out, norm_ffn_g, router_w, router_b,
           w_gate, b_gate, w_up, b_up, w_down, b_down):
    t, d = x2.shape
    aw = ATTN_HEADS * HEAD_DIM
    r = lru_lambda.shape[0]
    ne = router_w.shape[1]

    hm = _rmsnorm(x2, norm_mix_g, BF16)
    w_in_b = w_in.astype(BF16)
    qkv = _proj(hm, w_in_b, 0, 3 * aw, BF16, "scale_head",
                n_scaled_tiles=aw // 1024, scale=HEAD_DIM ** -0.5 * LOG2E)
    xr = _proj(hm, w_in_b, 3 * aw, r, F32)
    yr = _proj(hm, w_in_b, 3 * aw + r, r, F32, "gelu")
    gates = _proj(hm, w_in_b, 3 * aw + 2 * r, 2 * d, F32, "sigmoid")

    o = _moba(qkv.reshape(bsz, seq, 3 * aw), bsz, seq).reshape(t, aw)
    hy = _lru(xr.reshape(bsz, seq, r), yr.reshape(bsz, seq, r), conv_w, conv_b,
              lru_wa, lru_ba, lru_wx, lru_bx, lru_lambda, bsz, seq).reshape(t, r)

    merged = _merge(o, hy, w_attn_branch.astype(BF16), w_rnn_branch.astype(BF16), gates)
    x1, hf, logits = _outproj(merged, w_out.astype(BF16), x2, norm_ffn_g, router_w, router_b)

    gate_w, dest, block_expert, n_used, fill_start, p = _route(logits, ne)
    xs = _dispatch(hf, dest, fill_start, n_used, p, ne)
    y = _experts(xs, block_expert, n_used, w_gate, b_gate, w_up, b_up, w_down, b_down)
    return x1, y, dest, gate_w


def kernel(x, norm_mix_g, w_in, conv_w, conv_b, lru_wa, lru_ba, lru_wx, lru_bx, lru_lambda, w_attn_branch, w_rnn_branch, w_out, norm_ffn_g, router_w, router_b, w_gate, b_gate, w_up, b_up, w_down, b_down, norm_final_g):
    bsz, seq, d = x.shape
    depth = w_in.shape[0]
    assert seq % (2 * MOBA_BLOCK) == 0 and d % LANES == 0
    x2 = x.reshape(bsz * seq, d)
    for l in range(depth):
        x1, y, dest, gate_w = _layer(
            x2, bsz, seq, norm_mix_g[l], w_in[l], conv_w[l], conv_b[l], lru_wa[l], lru_ba[l],
            lru_wx[l], lru_bx[l], lru_lambda[l], w_attn_branch[l], w_rnn_branch[l], w_out[l],
            norm_ffn_g[l], router_w[l], router_b[l], w_gate[l], b_gate[l], w_up[l], b_up[l],
            w_down[l], b_down[l])
        last = l + 1 == depth
        x2 = _combine(x1, y, dest, gate_w, norm_final_g, norm=last)
    return x2.reshape(bsz, seq, d)
```

```python
import functools

import jax
import jax.numpy as jnp
from jax import lax
from jax.experimental import pallas as pl
from jax.experimental.pallas import tpu as pltpu

ATTN_HEADS = 16
HEAD_DIM = 128
MOBA_BLOCK = 256
MOBA_TOPK = 3
LRU_BLOCKS = 8
CONV_WIDTH = 4
LRU_C = 8.0
MOE_TOPK = 4
SWIGLU_ALPHA = 1.702
SWIGLU_LIMIT = 7.0
MOE_BLOCK = 512
RMS_EPS = 1e-6

LANES = 128
SUBLANES = 8
VMEM_LIMIT = 56 * 1024 * 1024
LOG2E = 1.4426950408889634
NEG_BIG = -1e30

BF16 = jnp.bfloat16
F32 = jnp.float32


def _cparams(sem):
    return pltpu.CompilerParams(dimension_semantics=sem, vmem_limit_bytes=VMEM_LIMIT)


def _sigmoid(x):
    return 1.0 / (1.0 + jnp.exp(-x))


def _gelu_tanh(x):
    c = 0.7978845608028654
    return 0.5 * x * (1.0 + jnp.tanh(c * (x + 0.044715 * (x * x * x))))


def _rmsnorm_kernel(x_ref, g_ref, o_ref):
    x = x_ref[...]
    ms = jnp.mean(x * x, axis=-1, keepdims=True)
    o_ref[...] = (x * lax.rsqrt(ms + RMS_EPS) * g_ref[...]).astype(o_ref.dtype)


def _rmsnorm(x, g, out_dtype, tm=1024):
    t, d = x.shape
    return pl.pallas_call(
        _rmsnorm_kernel,
        out_shape=jax.ShapeDtypeStruct((t, d), out_dtype),
        grid=(t // tm,),
        in_specs=[pl.BlockSpec((tm, d), lambda i: (i, 0)),
                  pl.BlockSpec((1, d), lambda i: (0, 0))],
        out_specs=pl.BlockSpec((tm, d), lambda i: (i, 0)),
        compiler_params=_cparams(("parallel",)),
        name="rmsnorm",
    )(x, g.reshape(1, d).astype(F32))


def _proj_kernel(a_ref, w_ref, o_ref, *, epilogue, n_scaled_tiles, scale):
    acc = jnp.dot(a_ref[...], w_ref[...], preferred_element_type=F32)
    if epilogue == "scale_head":
        s = jnp.where(pl.program_id(1) < n_scaled_tiles, scale, 1.0).astype(F32)
        acc = acc * s
    elif epilogue == "gelu":
        acc = _gelu_tanh(acc)
    elif epilogue == "sigmoid":
        acc = _sigmoid(acc)
    o_ref[...] = acc.astype(o_ref.dtype)


def _proj(a, w, col0, n, out_dtype, epilogue="none", n_scaled_tiles=0, scale=1.0,
          tm=1024, tn=1024):
    t, k = a.shape
    c0 = col0 // tn
    kern = functools.partial(_proj_kernel, epilogue=epilogue,
                             n_scaled_tiles=n_scaled_tiles, scale=scale)
    return pl.pallas_call(
        kern,
        out_shape=jax.ShapeDtypeStruct((t, n), out_dtype),
        grid=(t // tm, n // tn),
        in_specs=[pl.BlockSpec((tm, k), lambda i, j: (i, 0)),
                  pl.BlockSpec((k, tn), lambda i, j: (0, c0 + j))],
        out_specs=pl.BlockSpec((tm, tn), lambda i, j: (i, j)),
        compiler_params=_cparams(("parallel", "parallel")),
        name="proj_" + epilogue,
    )(a, w)


def _moba_kernel(q_ref, k_ref, v_ref, o_ref, kext_ref, vt_ref, s_ref, *, seq):
    blk = MOBA_BLOCK
    qt = 2 * blk
    nsb = seq // qt

    kext_ref[:, :HEAD_DIM] = k_ref[...]
    r_blk = lax.broadcasted_iota(jnp.int32, (seq, LANES), 0) // blk
    c_id = lax.broadcasted_iota(jnp.int32, (seq, LANES), 1)
    kext_ref[:, HEAD_DIM:] = jnp.where(r_blk == c_id, 1.0, 0.0).astype(BF16)
    vt_ref[...] = v_ref[...].T

    a_row = lax.broadcasted_iota(jnp.int32, (LANES, seq), 0)
    a_col = lax.broadcasted_iota(jnp.int32, (LANES, seq), 1) // blk
    avg = jnp.where(a_row == a_col, 1.0 / blk, 0.0).astype(BF16)
    kmean = jnp.dot(avg, k_ref[...], preferred_element_type=F32).astype(BF16)

    nrow = 32
    brow = lax.broadcasted_iota(jnp.int32, (nrow, qt), 0)
    second = lax.broadcasted_iota(jnp.int32, (nrow, qt), 1) >= blk
    key_i = lax.broadcasted_iota(jnp.int32, (qt, qt), 0)
    qry_i = lax.broadcasted_iota(jnp.int32, (qt, qt), 1)
    same_blk = (key_i >= blk) == (qry_i >= blk)
    causal_self = jnp.logical_and(same_blk, key_i <= qry_i)
    cross = jnp.logical_and(key_i < blk, qry_i >= blk)

    def q_super(a, _):
        row0 = pl.multiple_of(a * qt, qt)
        q = q_ref[pl.ds(row0, qt), :]
        q_t = q.T
        gate = jnp.dot(kmean, q_t, preferred_element_type=F32)[:nrow, :]
        qblk = 2 * a + second.astype(jnp.int32)
        past = brow < qblk
        g = jnp.where(past, gate, -jnp.inf)
        sel = jnp.zeros((nrow, qt), jnp.bool_)
        for _k in range(MOBA_TOPK):
            mx = jnp.max(g, axis=0, keepdims=True)
            first = jnp.min(jnp.where(g == mx, brow, nrow), axis=0, keepdims=True)
            pick = brow == first
            sel = jnp.logical_or(sel, pick)
            g = jnp.where(pick, -jnp.inf, g)
        chosen = jnp.logical_and(sel, past)
        bias_t = jnp.where(chosen, 0.0, NEG_BIG)
        bias_t = jnp.concatenate([bias_t, jnp.zeros((LANES - nrow, qt), F32)], axis=0)
        qext_t = jnp.concatenate([q_t, bias_t.astype(BF16)], axis=0)

        prev_sel = jnp.max(jnp.where(jnp.logical_and(chosen, brow == 2 * a), 1.0, 0.0),
                           axis=0, keepdims=True) > 0.5
        allowed = jnp.logical_or(causal_self, jnp.logical_and(cross, prev_sel))
        s_diag = jnp.where(allowed, jnp.dot(k_ref[pl.ds(row0, qt), :], q_t,
                                            preferred_element_type=F32), NEG_BIG)
        s_ref[0] = s_diag
        mx_diag = jnp.max(s_diag, axis=0, keepdims=True)

        def absorb(m, l, acc, slot, mx, blk_idx):
            c0 = pl.multiple_of(blk_idx * qt, qt)
            m_new = jnp.maximum(m, mx)
            alpha = jnp.exp2(m - m_new)
            p = jnp.exp2(s_ref[slot] - m_new)
            l = alpha * l + jnp.sum(p, axis=0, keepdims=True)
            acc = alpha * acc + jnp.dot(vt_ref[:, pl.ds(c0, qt)], p.astype(BF16),
                                        preferred_element_type=F32)
            return m_new, l, acc

        def kv_step(carry, b, slot):
            m, l, acc, mx_cur, blk_cur = carry
            c_next = pl.multiple_of(b * qt, qt)
            s_next = jnp.dot(kext_ref[pl.ds(c_next, qt), :], qext_t,
                             preferred_element_type=F32)
            s_ref[1 - slot] = s_next
            mx_next = jnp.max(s_next, axis=0, keepdims=True)
            m, l, acc = absorb(m, l, acc, slot, mx_cur, blk_cur)
            return m, l, acc, mx_next, b

        def kv_pair(b0, carry):
            return kv_step(kv_step(carry, b0, 0), b0 + 1, 1)

        def kv_quad(t, carry):
            return kv_pair(4 * t + 2, kv_pair(4 * t, carry))

        init = (jnp.full((1, qt), NEG_BIG, F32), jnp.zeros((1, qt), F32),
                jnp.zeros((HEAD_DIM, qt), F32), mx_diag, a)
        carry = lax.fori_loop(0, a // 4, kv_quad, init)
        carry = lax.cond(a % 4 >= 2, lambda c: kv_pair(a // 4 * 4, c), lambda c: c, carry)

        def odd_tail(carry):
            m, l, acc, mx_last, blk_last = kv_step(carry, a - 1, 0)
            return absorb(m, l, acc, 1, mx_last, blk_last)

        def even_tail(carry):
            m, l, acc, mx_last, blk_last = carry
            return absorb(m, l, acc, 0, mx_last, blk_last)

        m, l, acc = lax.cond(a % 2 == 1, odd_tail, even_tail, carry)
        o_ref[pl.ds(row0, qt), :] = (acc * (1.0 / l)).T.astype(o_ref.dtype)
        return 0

    lax.fori_loop(0, nsb, q_super, 0)


def _moba(qkv, bsz, seq):
    h = ATTN_HEADS
    kern = functools.partial(_moba_kernel, seq=seq)
    return pl.pallas_call(
        kern,
        out_shape=jax.ShapeDtypeStruct((bsz, seq, h * HEAD_DIM), BF16),
        grid=(bsz, h),
        in_specs=[pl.BlockSpec((None, seq, HEAD_DIM), lambda b, hh: (b, 0, hh)),
                  pl.BlockSpec((None, seq, HEAD_DIM), lambda b, hh: (b, 0, h + hh)),
                  pl.BlockSpec((None, seq, HEAD_DIM), lambda b, hh: (b, 0, 2 * h + hh))],
        out_specs=pl.BlockSpec((None, seq, HEAD_DIM), lambda b, hh: (b, 0, hh)),
        scratch_shapes=[pltpu.VMEM((seq, 2 * HEAD_DIM), BF16),
                        pltpu.VMEM((HEAD_DIM, seq), BF16),
                        pltpu.VMEM((2, 2 * MOBA_BLOCK, 2 * MOBA_BLOCK), F32)],
        compiler_params=_cparams(("parallel", "parallel")),
        name="moba_attention",
    )(qkv, qkv, qkv)


def _lru_kernel(x_ref, y_ref, cw_ref, cb_ref, w_ref, bias_ref, lam_ref, o_ref,
                win_ref, a_ref, b_ref, h_ref, *, ts):
    t_idx = pl.program_id(2)

    @pl.when(t_idx == 0)
    def _():
        win_ref[0:SUBLANES, :] = jnp.zeros((SUBLANES, win_ref.shape[1]), F32)
        h_ref[...] = jnp.zeros_like(h_ref)

    win_ref[SUBLANES:, :] = x_ref[...]
    xc = cb_ref[...] + cw_ref[CONV_WIDTH - 1:CONV_WIDTH, :] * x_ref[...]
    for k in range(CONV_WIDTH - 1):
        shift = CONV_WIDTH - 1 - k
        xc = xc + cw_ref[k:k + 1, :] * win_ref[SUBLANES - shift:SUBLANES - shift + ts, :]
    win_ref[0:SUBLANES, :] = win_ref[ts:ts + SUBLANES, :]

    gates = jnp.dot(xc.astype(BF16), w_ref[...], preferred_element_type=F32) + bias_ref[...]
    cblk = xc.shape[1]
    rt = _sigmoid(gates[:, :cblk])
    it = _sigmoid(gates[:, cblk:])
    neg_lam = -lam_ref[...]
    softplus = jnp.maximum(neg_lam, 0.0) + jnp.log1p(jnp.exp(-jnp.abs(neg_lam)))
    log_a = (-LRU_C) * rt * softplus
    a = jnp.exp(log_a)
    a_ref[...] = a
    b_ref[...] = jnp.sqrt(-jnp.tanh(log_a) * (a * a + 1.0)) * (it * xc)

    row = lax.broadcasted_iota(jnp.int32, (SUBLANES, cblk), 0)

    def group(gi, h_in):
        r0 = pl.multiple_of(gi * SUBLANES, SUBLANES)
        a = a_ref[pl.ds(r0, SUBLANES), :]
        b = b_ref[pl.ds(r0, SUBLANES), :]
        for d in (1, 2, 4):
            keep = row >= d
            a_sh = jnp.where(keep, pltpu.roll(a, d, axis=0), 1.0)
            b_sh = jnp.where(keep, pltpu.roll(b, d, axis=0), 0.0)
            b = a * b_sh + b
            a = a * a_sh
        hs = b + a * h_in
        o_ref[pl.ds(r0, SUBLANES), :] = (hs * y_ref[pl.ds(r0, SUBLANES), :]).astype(o_ref.dtype)
        return jnp.broadcast_to(hs[SUBLANES - 1:SUBLANES, :], (SUBLANES, cblk))

    h_ref[...] = lax.fori_loop(0, ts // SUBLANES, group, h_ref[...], unroll=4)


def _lru(xr, yr, conv_w, conv_b, wa, ba, wx, bx, lam, bsz, seq, ts=1024):
    r = xr.shape[-1]
    nblk = LRU_BLOCKS
    cblk = r // nblk
    w_cat = jnp.concatenate([wa, wx], axis=-1).astype(BF16)
    b_cat = jnp.concatenate([ba.reshape(nblk, 1, cblk), bx.reshape(nblk, 1, cblk)], axis=-1).astype(F32)
    kern = functools.partial(_lru_kernel, ts=ts)
    chan = lambda b, n, t: (0, n)
    return pl.pallas_call(
        kern,
        out_shape=jax.ShapeDtypeStruct((bsz, seq, r), BF16),
        grid=(bsz, nblk, seq // ts),
        in_specs=[pl.BlockSpec((None, ts, cblk), lambda b, n, t: (b, t, n)),
                  pl.BlockSpec((None, ts, cblk), lambda b, n, t: (b, t, n)),
                  pl.BlockSpec((CONV_WIDTH, cblk), chan),
                  pl.BlockSpec((1, cblk), chan),
                  pl.BlockSpec((None, cblk, 2 * cblk), lambda b, n, t: (n, 0, 0)),
                  pl.BlockSpec((None, 1, 2 * cblk), lambda b, n, t: (n, 0, 0)),
                  pl.BlockSpec((1, cblk), chan)],
        out_specs=pl.BlockSpec((None, ts, cblk), lambda b, n, t: (b, t, n)),
        scratch_shapes=[pltpu.VMEM((ts + SUBLANES, cblk), F32),
                        pltpu.VMEM((ts, cblk), F32),
                        pltpu.VMEM((ts, cblk), F32),
                        pltpu.VMEM((SUBLANES, cblk), F32)],
        compiler_params=_cparams(("parallel", "parallel", "arbitrary")),
        name="conv_rglru",
    )(xr, yr, conv_w.astype(F32), conv_b.reshape(1, r).astype(F32), w_cat, b_cat,
      lam.reshape(1, r).astype(F32))


def _merge_kernel(o_ref, hy_ref, wa_ref, wr_ref, ga_ref, gr_ref, out_ref):
    ya = jnp.dot(o_ref[...], wa_ref[...], preferred_element_type=F32)
    yr = jnp.dot(hy_ref[...], wr_ref[...], preferred_element_type=F32)
    out_ref[...] = (ga_ref[...] * ya + gr_ref[...] * yr).astype(out_ref.dtype)


def _merge(o, hy, wa, wr, gates, tm=1024, tn=512):
    t, k = o.shape
    kr = hy.shape[1]
    n = wa.shape[1]
    nj = n // tn
    return pl.pallas_call(
        _merge_kernel,
        out_shape=jax.ShapeDtypeStruct((t, n), BF16),
        grid=(t // tm, nj),
        in_specs=[pl.BlockSpec((tm, k), lambda i, j: (i, 0)),
                  pl.BlockSpec((tm, kr), lambda i, j: (i, 0)),
                  pl.BlockSpec((k, tn), lambda i, j: (0, j)),
                  pl.BlockSpec((kr, tn), lambda i, j: (0, j)),
                  pl.BlockSpec((tm, tn), lambda i, j: (i, j)),
                  pl.BlockSpec((tm, tn), lambda i, j: (i, nj + j))],
        out_specs=pl.BlockSpec((tm, tn), lambda i, j: (i, j)),
        compiler_params=_cparams(("parallel", "parallel")),
        name="branch_merge",
    )(o, hy, wa, wr, gates, gates)


def _pack_halves(x):
    bits = pltpu.bitcast(x.astype(F32), jnp.uint32)
    half = x.shape[1] // 2
    return (bits[:, :half] >> 16) | bits[:, half:]


def _unpack_halves(w):
    lo = pltpu.bitcast(w << 16, F32).astype(BF16)
    hi = pltpu.bitcast(w & jnp.uint32(0xFFFF0000), F32).astype(BF16)
    return jnp.concatenate([lo, hi], axis=1)


def _outproj_kernel(m_ref, wo_ref, x_ref, g_ref, rw_ref, rb_ref, x1_ref, hf_ref, lg_ref):
    x1 = x_ref[...] + jnp.dot(m_ref[...], wo_ref[...], preferred_element_type=F32)
    x1_ref[...] = x1
    ms = jnp.mean(x1 * x1, axis=-1, keepdims=True)
    hf = (x1 * lax.rsqrt(ms + RMS_EPS) * g_ref[...]).astype(BF16)
    hf_ref[...] = _pack_halves(hf)
    lg_ref[...] = jnp.dot(hf, rw_ref[...], preferred_element_type=F32) + rb_ref[...]


def _outproj(merged, wo, x, g, rw, rb, tm=512):
    t, d = x.shape
    ne = rw.shape[1]
    rw_p = jnp.zeros((d, LANES), BF16).at[:, :ne].set(rw.astype(BF16))
    rb_p = jnp.zeros((1, LANES), F32).at[0, :ne].set(rb.astype(F32))
    row = lambda i: (i, 0)
    fixed = lambda i: (0, 0)
    return pl.pallas_call(
        _outproj_kernel,
        out_shape=(jax.ShapeDtypeStruct((t, d), F32),
                   jax.ShapeDtypeStruct((t, d // 2), jnp.uint32),
                   jax.ShapeDtypeStruct((t, LANES), F32)),
        grid=(t // tm,),
        in_specs=[pl.BlockSpec((tm, d), row),
                  pl.BlockSpec((d, d), fixed),
                  pl.BlockSpec((tm, d), row),
                  pl.BlockSpec((1, d), fixed),
                  pl.BlockSpec((d, LANES), fixed),
                  pl.BlockSpec((1, LANES), fixed)],
        out_specs=(pl.BlockSpec((tm, d), row),
                   pl.BlockSpec((tm, d // 2), row),
                   pl.BlockSpec((tm, LANES), row)),
        compiler_params=_cparams(("parallel",)),
        name="outproj_norm_router",
    )(merged, wo, x, g.reshape(1, d).astype(F32), rw_p, rb_p)


def _expert_changed(be_ref, r):
    return jnp.logical_or(r == 0, be_ref[r] != be_ref[jnp.maximum(r - 1, 0)])


def _expert_up_kernel(be_ref, nu_ref, xs_ref, wg_ref, bg_ref, wu_ref, bu_ref, act_ref,
                      wgb_ref, wub_ref):
    r = pl.program_id(1)
    used = r < nu_ref[0]

    @pl.when(jnp.logical_and(used, _expert_changed(be_ref, r)))
    def _():
        wgb_ref[...] = wg_ref[...].astype(BF16)
        wub_ref[...] = wu_ref[...].astype(BF16)

    @pl.when(used)
    def _():
        xs = _unpack_halves(xs_ref[...])
        glu = jnp.minimum(jnp.dot(xs, wgb_ref[...], preferred_element_type=F32) + bg_ref[...],
                          SWIGLU_LIMIT)
        lin = jnp.clip(jnp.dot(xs, wub_ref[...], preferred_element_type=F32) + bu_ref[...],
                       -SWIGLU_LIMIT, SWIGLU_LIMIT)
        act_ref[...] = (glu * _sigmoid(SWIGLU_ALPHA * glu) * (lin + 1.0)).astype(act_ref.dtype)

    @pl.when(jnp.logical_not(used))
    def _():
        act_ref[...] = jnp.zeros_like(act_ref)


def _expert_down_kernel(be_ref, nu_ref, act_ref, wd_ref, bd_ref, y_ref, wdb_ref):
    r = pl.program_id(1)
    used = r < nu_ref[0]

    @pl.when(jnp.logical_and(used, _expert_changed(be_ref, r)))
    def _():
        wdb_ref[...] = wd_ref[...].astype(BF16)

    @pl.when(used)
    def _():
        y_ref[...] = jnp.dot(act_ref[...], wdb_ref[...], preferred_element_type=F32) + bd_ref[...]

    @pl.when(jnp.logical_not(used))
    def _():
        y_ref[...] = jnp.zeros_like(y_ref)


def _experts(xs, block_expert, n_used, wg, bg, wu, bu, wd, bd, tf=1024, tn=2048):
    p = xs.shape[0]
    ne, d, dff = wg.shape
    tf, tn = min(tf, dff), min(tn, d)
    nblk = p // MOE_BLOCK

    def rr(r, nu):
        return jnp.minimum(r, nu[0] - 1)

    wspec = lambda k, t: pl.BlockSpec((None, k, t), lambda c, r, be, nu: (be[rr(r, nu)], 0, c))
    bspec = lambda t: pl.BlockSpec((None, 1, t), lambda c, r, be, nu: (be[rr(r, nu)], 0, c))
    rows = lambda k: pl.BlockSpec((MOE_BLOCK, k), lambda c, r, be, nu: (rr(r, nu), 0))

    act = pl.pallas_call(
        _expert_up_kernel,
        out_shape=jax.ShapeDtypeStruct((p, dff), BF16),
        grid_spec=pltpu.PrefetchScalarGridSpec(
            num_scalar_prefetch=2,
            grid=(dff // tf, nblk),
            in_specs=[rows(d // 2), wspec(d, tf), bspec(tf), wspec(d, tf), bspec(tf)],
            out_specs=pl.BlockSpec((MOE_BLOCK, tf), lambda c, r, be, nu: (r, c)),
            scratch_shapes=[pltpu.VMEM((d, tf), BF16), pltpu.VMEM((d, tf), BF16)]),
        compiler_params=_cparams(("arbitrary", "arbitrary")),
        name="moe_up",
    )(block_expert, n_used, xs, wg, bg.reshape(ne, 1, dff).astype(F32), wu,
      bu.reshape(ne, 1, dff).astype(F32))

    return pl.pallas_call(
        _expert_down_kernel,
        out_shape=jax.ShapeDtypeStruct((p, d), F32),
        grid_spec=pltpu.PrefetchScalarGridSpec(
            num_scalar_prefetch=2,
            grid=(d // tn, nblk),
            in_specs=[rows(dff), wspec(dff, tn), bspec(tn)],
            out_specs=pl.BlockSpec((MOE_BLOCK, tn), lambda c, r, be, nu: (r, c)),
            scratch_shapes=[pltpu.VMEM((dff, tn), BF16)]),
        compiler_params=_cparams(("arbitrary", "arbitrary")),
        name="moe_down",
    )(block_expert, n_used, act, wd, bd.reshape(ne, 1, d).astype(F32))


def _combine_kernel(dest_ref, x1_ref, gw_ref, g_ref, y_hbm, o_ref, *scratch, tt, ct, norm):
    nch = tt // ct
    bufs, sem_ref = scratch[:nch], scratch[nch]
    i = pl.program_id(0)
    last = pl.num_programs(0) - 1
    lead = 2
    grp = 2 * SUBLANES

    def issue_rows(tok0, n, slot, asg0):
        for j in range(n):
            for k in range(MOE_TOPK):
                row = dest_ref[asg0 + j * MOE_TOPK + k]
                pltpu.make_async_copy(y_hbm.at[pl.ds(row, 1), :],
                                      bufs[slot].at[k, pl.ds(tok0 + j, 1), :],
                                      sem_ref.at[slot]).start()

    def issue_chunk(step, c):
        asg = (step * tt + c * ct) * MOE_TOPK

        def body(g, _):
            tok0 = pl.multiple_of(g * grp, grp)
            issue_rows(tok0, grp, c, asg + tok0 * MOE_TOPK)
            return 0
        lax.fori_loop(0, ct // grp, body, 0)

    def reduce_group(c, tok0):
        rows = pl.ds(c * ct + tok0, grp)
        acc = x1_ref[rows, :]
        gw = gw_ref[rows, :]
        for k in range(MOE_TOPK):
            acc = acc + gw[:, k:k + 1] * bufs[c][k, pl.ds(tok0, grp), :]
        if norm:
            ms = jnp.mean(acc * acc, axis=-1, keepdims=True)
            acc = acc * lax.rsqrt(ms + RMS_EPS) * g_ref[...]
        o_ref[rows, :] = acc

    @pl.when(i == 0)
    def _():
        for c in range(lead):
            issue_chunk(0, c)

    for c in range(nch):
        for k in range(MOE_TOPK):
            pltpu.make_async_copy(y_hbm.at[pl.ds(0, ct), :], bufs[c].at[k], sem_ref.at[c]).wait()
        c_next = (c + lead) % nch
        step_next = i + (c + lead) // nch

        def reduce_and_issue(g, _, c=c, c_next=c_next, step_next=step_next):
            tok0 = pl.multiple_of(g * grp, grp)
            issue_rows(tok0, grp, c_next, ((step_next * tt + c_next * ct) + tok0) * MOE_TOPK)
            reduce_group(c, tok0)
            return 0

        def reduce_only(g, _, c=c):
            reduce_group(c, pl.multiple_of(g * grp, grp))
            return 0

        if c + lead < nch:
            lax.fori_loop(0, ct // grp, reduce_and_issue, 0)
        else:
            @pl.when(i < last)
            def _(f=reduce_and_issue):
                lax.fori_loop(0, ct // grp, f, 0)

            @pl.when(i == last)
            def _(f=reduce_only):
                lax.fori_loop(0, ct // grp, f, 0)


def _combine(x1, y, dest, gate_w, g, norm, tt=256, ct=64):
    t, d = x1.shape
    nch = tt // ct
    kern = functools.partial(_combine_kernel, tt=tt, ct=ct, norm=norm)
    return pl.pallas_call(
        kern,
        out_shape=jax.ShapeDtypeStruct((t, d), F32),
        grid_spec=pltpu.PrefetchScalarGridSpec(
            num_scalar_prefetch=1,
            grid=(t // tt,),
            in_specs=[pl.BlockSpec((tt, d), lambda i, dref: (i, 0)),
                      pl.BlockSpec((tt, MOE_TOPK), lambda i, dref: (i, 0)),
                      pl.BlockSpec((1, d), lambda i, dref: (0, 0)),
                      pl.BlockSpec(memory_space=pl.ANY)],
            out_specs=pl.BlockSpec((tt, d), lambda i, dref: (i, 0)),
            scratch_shapes=[pltpu.VMEM((MOE_TOPK, ct, d), F32) for _ in range(nch)]
            + [pltpu.SemaphoreType.DMA((nch,))]),
        compiler_params=_cparams(("arbitrary",)),
        name="moe_combine",
    )(dest.reshape(t * MOE_TOPK), x1, gate_w, g.reshape(1, d).astype(F32), y)


def _route_kernel(lg_ref, out_ref, cnt_ref, base_ref, *, n_experts, tm):
    @pl.when(pl.program_id(0) == 0)
    def _():
        base_ref[...] = jnp.zeros_like(base_ref)

    lane = lax.broadcasted_iota(jnp.int32, (tm, LANES), 1)
    lg = jnp.where(lane < n_experts, lg_ref[...], -jnp.inf)
    vals, ids = [], []
    onehot = jnp.zeros((tm, LANES), jnp.bool_)
    for _k in range(MOE_TOPK):
        mx = jnp.max(lg, axis=1, keepdims=True)
        first = jnp.min(jnp.where(lg == mx, lane, LANES), axis=1, keepdims=True)
        pick = lane == first
        onehot = jnp.logical_or(onehot, pick)
        lg = jnp.where(pick, -jnp.inf, lg)
        vals.append(mx)
        ids.append(first)
    exps = [jnp.exp(v - vals[0]) for v in vals]
    denom = exps[0] + exps[1] + exps[2] + exps[3]

    oh = jnp.where(onehot, 1.0, 0.0)
    r_i = lax.broadcasted_iota(jnp.int32, (tm, tm), 0)
    c_i = lax.broadcasted_iota(jnp.int32, (tm, tm), 1)
    earlier = jnp.where(r_i > c_i, 1.0, 0.0).astype(BF16)
    before = base_ref[...] + jnp.dot(earlier, oh.astype(BF16), preferred_element_type=F32)
    base_ref[...] = base_ref[...] + jnp.sum(oh, axis=0, keepdims=True)
    cnt_ref[...] = base_ref[...]

    out = jnp.zeros((tm, LANES), F32)
    for k in range(MOE_TOPK):
        rank = jnp.sum(jnp.where(lane == ids[k], before, 0.0), axis=1, keepdims=True)
        out = jnp.where(lane == k, exps[k] / denom, out)
        out = jnp.where(lane == MOE_TOPK + k, ids[k].astype(F32), out)
        out = jnp.where(lane == 2 * MOE_TOPK + k, rank, out)
    out_ref[...] = out


def _route(logits, n_experts, tm=512):
    t = logits.shape[0]
    kern = functools.partial(_route_kernel, n_experts=n_experts, tm=tm)
    packed, counts = pl.pallas_call(
        kern,
        out_shape=(jax.ShapeDtypeStruct((t, LANES), F32), jax.ShapeDtypeStruct((1, LANES), F32)),
        grid=(t // tm,),
        in_specs=[pl.BlockSpec((tm, LANES), lambda i: (i, 0))],
        out_specs=(pl.BlockSpec((tm, LANES), lambda i: (i, 0)),
                   pl.BlockSpec((1, LANES), lambda i: (0, 0))),
        scratch_shapes=[pltpu.VMEM((1, LANES), F32)],
        compiler_params=_cparams(("arbitrary",)),
        name="moe_route",
    )(logits)
    blk = MOE_BLOCK
    n_asg = t * MOE_TOPK
    p = (-(-n_asg // blk)) * blk + n_experts * blk
    cnt = counts[0, :n_experts].astype(jnp.int32)
    padded = (cnt + blk - 1) // blk * blk
    pad_end = jnp.cumsum(padded)
    pad_start = pad_end - padded
    row_starts = jnp.arange(p // blk, dtype=jnp.int32) * blk
    block_expert = jnp.minimum(jnp.sum(pad_end[None, :] <= row_starts[:, None], axis=1),
                               n_experts - 1).astype(jnp.int32)
    n_used = (pad_end[-1] // blk).astype(jnp.int32).reshape(1)
    gate_w = packed[:, :MOE_TOPK]
    ids = packed[:, MOE_TOPK:2 * MOE_TOPK].astype(jnp.int32)
    rank = packed[:, 2 * MOE_TOPK:3 * MOE_TOPK].astype(jnp.int32)
    sel = ids[:, :, None] == jnp.arange(n_experts, dtype=jnp.int32)[None, None, :]
    dest = rank + jnp.sum(jnp.where(sel, pad_start[None, None, :], 0), axis=-1)
    fill_start = jnp.minimum((pad_start + cnt) // SUBLANES * SUBLANES, p - blk).astype(jnp.int32)
    return gate_w, dest, block_expert, n_used, fill_start, p


def _dispatch_kernel(dest_ref, fill_ref, nu_ref, hf_ref, xs_hbm, zero_ref, sem_ref,
                     *, tt, n_experts, nblk):
    i = pl.program_id(0)

    def zero_fill(start):
        return pltpu.make_async_copy(zero_ref, xs_hbm.at[pl.ds(start, MOE_BLOCK), :],
                                     sem_ref.at[0])

    @pl.when(i == 0)
    def _():
        zero_ref[...] = jnp.zeros_like(zero_ref)
        for e in range(n_experts):
            fill = zero_fill(pl.multiple_of(fill_ref[e], SUBLANES))
            fill.start()
            fill.wait()

        def fill_block(r, _):
            fill = zero_fill(pl.multiple_of(r * MOE_BLOCK, MOE_BLOCK))
            fill.start()
            fill.wait()
            return 0

        lax.fori_loop(nu_ref[0], nblk, fill_block, 0)

    base = i * tt

    def body(tok, _):
        for k in range(MOE_TOPK):
            row = dest_ref[(base + tok) * MOE_TOPK + k]
            pltpu.make_async_copy(hf_ref.at[pl.ds(tok, 1), :],
                                  xs_hbm.at[pl.ds(row, 1), :], sem_ref.at[1]).start()
        return 0

    lax.fori_loop(0, tt, body, 0, unroll=2)
    for k in range(MOE_TOPK):
        pltpu.make_async_copy(hf_ref, xs_hbm.at[pl.ds(0, tt), :], sem_ref.at[1]).wait()


def _dispatch(hfp, dest, fill_start, n_used, p, n_experts, tt=512):
    t, w = hfp.shape
    kern = functools.partial(_dispatch_kernel, tt=tt, n_experts=n_experts, nblk=p // MOE_BLOCK)
    return pl.pallas_call(
        kern,
        out_shape=jax.ShapeDtypeStruct((p, w), hfp.dtype),
        grid_spec=pltpu.PrefetchScalarGridSpec(
            num_scalar_prefetch=3,
            grid=(t // tt,),
            in_specs=[pl.BlockSpec((tt, w), lambda i, dref, fref, nref: (i, 0))],
            out_specs=pl.BlockSpec(memory_space=pl.ANY),
            scratch_shapes=[pltpu.VMEM((MOE_BLOCK, w), hfp.dtype),
                            pltpu.SemaphoreType.DMA((2,))]),
        compiler_params=_cparams(("arbitrary",)),
        name="moe_dispatch",
    )(dest.reshape(t * MOE_TOPK), fill_start, n_used, hfp)


def _layer(x2, bsz, seq, norm_mix_g, w_in, conv_w, conv_b, lru_wa, lru_ba, lru_wx, lru_bx,
           lru_lambda, w_attn_branch, w_rnn_branch, w_out, norm_ffn_g, router_w, router_b,
           w_gate, b_gate, w_up, b_up, w_down, b_down):
    t, d = x2.shape
    aw = ATTN_HEADS * HEAD_DIM
    r = lru_lambda.shape[0]
    ne = router_w.shape[1]

    hm = _rmsnorm(x2, norm_mix_g, BF16)
    w_in_b = w_in.astype(BF16)
    qkv = _proj(hm, w_in_b, 0, 3 * aw, BF16, "scale_head",
                n_scaled_tiles=aw // 1024, scale=HEAD_DIM ** -0.5 * LOG2E)
    xr = _proj(hm, w_in_b, 3 * aw, r, F32)
    yr = _proj(hm, w_in_b, 3 * aw + r, r, F32, "gelu")
    gates = _proj(hm, w_in_b, 3 * aw + 2 * r, 2 * d, F32, "sigmoid")

    o = _moba(qkv.reshape(bsz, seq, 3 * aw), bsz, seq).reshape(t, aw)
    hy = _lru(xr.reshape(bsz, seq, r), yr.reshape(bsz, seq, r), conv_w, conv_b,
              lru_wa, lru_ba, lru_wx, lru_bx, lru_lambda, bsz, seq).reshape(t, r)

    merged = _merge(o, hy, w_attn_branch.astype(BF16), w_rnn_branch.astype(BF16), gates)
    x1, hf, logits = _outproj(merged, w_out.astype(BF16), x2, norm_ffn_g, router_w, router_b)

    gate_w, dest, block_expert, n_used, fill_start, p = _route(logits, ne)
    xs = _dispatch(hf, dest, fill_start, n_used, p, ne)
    y = _experts(xs, block_expert, n_used, w_gate, b_gate, w_up, b_up, w_down, b_down)
    return x1, y, dest, gate_w


def kernel(x, norm_mix_g, w_in, conv_w, conv_b, lru_wa, lru_ba, lru_wx, lru_bx, lru_lambda, w_attn_branch, w_rnn_branch, w_out, norm_ffn_g, router_w, router_b, w_gate, b_gate, w_up, b_up, w_down, b_down, norm_final_g):
    bsz, seq, d = x.shape
    depth = w_in.shape[0]
    assert seq % (2 * MOBA_BLOCK) == 0 and d % LANES == 0
    x2 = x.reshape(bsz * seq, d)
    for l in range(depth):
        x1, y, dest, gate_w = _layer(
            x2, bsz, seq, norm_mix_g[l], w_in[l], conv_w[l], conv_b[l], lru_wa[l], lru_ba[l],
            lru_wx[l], lru_bx[l], lru_lambda[l], w_attn_branch[l], w_rnn_branch[l], w_out[l],
            norm_ffn_g[l], router_w[l], router_b[l], w_gate[l], b_gate[l], w_up[l], b_up[l],
            w_down[l], b_down[l])
        last = l + 1 == depth
        x2 = _combine(x1, y, dest, gate_w, norm_final_g, norm=last)
    return x2.reshape(bsz, seq, d)
```

```python
import functools

import jax
import jax.numpy as jnp
from jax import lax
from jax.experimental import pallas as pl
from jax.experimental.pallas import tpu as pltpu

ATTN_HEADS = 16
HEAD_DIM = 128
MOBA_BLOCK = 256
MOBA_TOPK = 3
LRU_BLOCKS = 8
CONV_WIDTH = 4
LRU_C = 8.0
MOE_TOPK = 4
SWIGLU_ALPHA = 1.702
SWIGLU_LIMIT = 7.0
MOE_BLOCK = 512
RMS_EPS = 1e-6

LANES = 128
SUBLANES = 8
VMEM_LIMIT = 56 * 1024 * 1024
LOG2E = 1.4426950408889634
NEG_BIG = -1e30

BF16 = jnp.bfloat16
F32 = jnp.float32


def _cparams(sem):
    return pltpu.CompilerParams(dimension_semantics=sem, vmem_limit_bytes=VMEM_LIMIT)


def _sigmoid(x):
    return 1.0 / (1.0 + jnp.exp(-x))


def _gelu_tanh(x):
    c = 0.7978845608028654
    return 0.5 * x * (1.0 + jnp.tanh(c * (x + 0.044715 * (x * x * x))))


def _rmsnorm_kernel(x_ref, g_ref, o_ref):
    x = x_ref[...]
    ms = jnp.mean(x * x, axis=-1, keepdims=True)
    o_ref[...] = (x * lax.rsqrt(ms + RMS_EPS) * g_ref[...]).astype(o_ref.dtype)


def _rmsnorm(x, g, out_dtype, tm=1024):
    t, d = x.shape
    return pl.pallas_call(
        _rmsnorm_kernel,
        out_shape=jax.ShapeDtypeStruct((t, d), out_dtype),
        grid=(t // tm,),
        in_specs=[pl.BlockSpec((tm, d), lambda i: (i, 0)),
                  pl.BlockSpec((1, d), lambda i: (0, 0))],
        out_specs=pl.BlockSpec((tm, d), lambda i: (i, 0)),
        compiler_params=_cparams(("parallel",)),
        name="rmsnorm",
    )(x, g.reshape(1, d).astype(F32))


def _proj_kernel(a_ref, w_ref, o_ref, *, epilogue, n_scaled_tiles, scale):
    acc = jnp.dot(a_ref[...], w_ref[...], preferred_element_type=F32)
    if epilogue == "scale_head":
        s = jnp.where(pl.program_id(1) < n_scaled_tiles, scale, 1.0).astype(F32)
        acc = acc * s
    elif epilogue == "gelu":
        acc = _gelu_tanh(acc)
    elif epilogue == "sigmoid":
        acc = _sigmoid(acc)
    o_ref[...] = acc.astype(o_ref.dtype)


def _proj(a, w, col0, n, out_dtype, epilogue="none", n_scaled_tiles=0, scale=1.0,
          tm=1024, tn=1024):
    t, k = a.shape
    c0 = col0 // tn
    kern = functools.partial(_proj_kernel, epilogue=epilogue,
                             n_scaled_tiles=n_scaled_tiles, scale=scale)
    return pl.pallas_call(
        kern,
        out_shape=jax.ShapeDtypeStruct((t, n), out_dtype),
        grid=(t // tm, n // tn),
        in_specs=[pl.BlockSpec((tm, k), lambda i, j: (i, 0)),
                  pl.BlockSpec((k, tn), lambda i, j: (0, c0 + j))],
        out_specs=pl.BlockSpec((tm, tn), lambda i, j: (i, j)),
        compiler_params=_cparams(("parallel", "parallel")),
        name="proj_" + epilogue,
    )(a, w)


def _moba_kernel(q_ref, k_ref, v_ref, o_ref, kext_ref, vt_ref, s_ref, avg_ref, *, seq):
    blk = MOBA_BLOCK
    qt = 2 * blk
    nsb = seq // qt

    @pl.when(jnp.logical_and(pl.program_id(0) == 0, pl.program_id(1) == 0))
    def _():
        r_blk = lax.broadcasted_iota(jnp.int32, (seq, LANES), 0) // blk
        c_id = lax.broadcasted_iota(jnp.int32, (seq, LANES), 1)
        kext_ref[:, HEAD_DIM:] = jnp.where(r_blk == c_id, 1.0, 0.0).astype(BF16)
        a_row = lax.broadcasted_iota(jnp.int32, (LANES, seq), 0)
        a_col = lax.broadcasted_iota(jnp.int32, (LANES, seq), 1) // blk
        avg_ref[...] = jnp.where(a_row == a_col, 1.0 / blk, 0.0).astype(BF16)

    kext_ref[:, :HEAD_DIM] = k_ref[...]
    vt_ref[...] = v_ref[...].T
    kmean = jnp.dot(avg_ref[...], k_ref[...], preferred_element_type=F32).astype(BF16)

    nrow = 32
    brow = lax.broadcasted_iota(jnp.int32, (nrow, qt), 0)
    second = lax.broadcasted_iota(jnp.int32, (nrow, qt), 1) >= blk
    key_i = lax.broadcasted_iota(jnp.int32, (qt, qt), 0)
    qry_i = lax.broadcasted_iota(jnp.int32, (qt, qt), 1)
    same_blk = (key_i >= blk) == (qry_i >= blk)
    causal_self = jnp.logical_and(same_blk, key_i <= qry_i)
    cross = jnp.logical_and(key_i < blk, qry_i >= blk)

    def q_super(a, _):
        row0 = pl.multiple_of(a * qt, qt)
        q = q_ref[pl.ds(row0, qt), :]
        q_t = q.T
        gate = jnp.dot(kmean, q_t, preferred_element_type=F32)[:nrow, :]
        qblk = 2 * a + second.astype(jnp.int32)
        past = brow < qblk
        g = jnp.where(past, gate, -jnp.inf)
        sel = jnp.zeros((nrow, qt), jnp.bool_)
        for _k in range(MOBA_TOPK):
            mx = jnp.max(g, axis=0, keepdims=True)
            first = jnp.min(jnp.where(g == mx, brow, nrow), axis=0, keepdims=True)
            pick = brow == first
            sel = jnp.logical_or(sel, pick)
            g = jnp.where(pick, -jnp.inf, g)
        chosen = jnp.logical_and(sel, past)
        bias_t = jnp.where(chosen, 0.0, NEG_BIG)
        bias_t = jnp.concatenate([bias_t, jnp.zeros((LANES - nrow, qt), F32)], axis=0)
        qext_t = jnp.concatenate([q_t, bias_t.astype(BF16)], axis=0)

        prev_sel = jnp.max(jnp.where(jnp.logical_and(chosen, brow == 2 * a), 1.0, 0.0),
                           axis=0, keepdims=True) > 0.5
        allowed = jnp.logical_or(causal_self, jnp.logical_and(cross, prev_sel))
        s_diag = jnp.where(allowed, jnp.dot(k_ref[pl.ds(row0, qt), :], q_t,
                                            preferred_element_type=F32), NEG_BIG)
        s_ref[0] = s_diag
        mx_diag = jnp.max(s_diag, axis=0, keepdims=True)

        def absorb(m, l, acc, slot, mx, blk_idx):
            c0 = pl.multiple_of(blk_idx * qt, qt)
            m_new = jnp.maximum(m, mx)
            alpha = jnp.exp2(m - m_new)
            p = jnp.exp2(s_ref[slot] - m_new)
            l = alpha * l + jnp.sum(p, axis=0, keepdims=True)
            acc = alpha * acc + jnp.dot(vt_ref[:, pl.ds(c0, qt)], p.astype(BF16),
                                        preferred_element_type=F32)
            return m_new, l, acc

        def kv_step(carry, b, slot):
            m, l, acc, mx_cur, blk_cur = carry
            c_next = pl.multiple_of(b * qt, qt)
            s_next = jnp.dot(kext_ref[pl.ds(c_next, qt), :], qext_t,
                             preferred_element_type=F32)
            s_ref[1 - slot] = s_next
            mx_next = jnp.max(s_next, axis=0, keepdims=True)
            m, l, acc = absorb(m, l, acc, slot, mx_cur, blk_cur)
            return m, l, acc, mx_next, b

        def kv_pair(b0, carry):
            return kv_step(kv_step(carry, b0, 0), b0 + 1, 1)

        def kv_quad(t, carry):
            return kv_pair(4 * t + 2, kv_pair(4 * t, carry))

        init = (jnp.full((1, qt), NEG_BIG, F32), jnp.zeros((1, qt), F32),
                jnp.zeros((HEAD_DIM, qt), F32), mx_diag, a)
        carry = lax.fori_loop(0, a // 4, kv_quad, init)
        carry = lax.cond(a % 4 >= 2, lambda c: kv_pair(a // 4 * 4, c), lambda c: c, carry)

        def odd_tail(carry):
            m, l, acc, mx_last, blk_last = kv_step(carry, a - 1, 0)
            return absorb(m, l, acc, 1, mx_last, blk_last)

        def even_tail(carry):
            m, l, acc, mx_last, blk_last = carry
            return absorb(m, l, acc, 0, mx_last, blk_last)

        m, l, acc = lax.cond(a % 2 == 1, odd_tail, even_tail, carry)
        o_ref[pl.ds(row0, qt), :] = (acc * (1.0 / l)).T.astype(o_ref.dtype)
        return 0

    lax.fori_loop(0, nsb, q_super, 0)


def _moba(qkv, bsz, seq):
    h = ATTN_HEADS
    kern = functools.partial(_moba_kernel, seq=seq)
    return pl.pallas_call(
        kern,
        out_shape=jax.ShapeDtypeStruct((bsz, seq, h * HEAD_DIM), BF16),
        grid=(bsz, h),
        in_specs=[pl.BlockSpec((None, seq, HEAD_DIM), lambda b, hh: (b, 0, hh)),
                  pl.BlockSpec((None, seq, HEAD_DIM), lambda b, hh: (b, 0, h + hh)),
                  pl.BlockSpec((None, seq, HEAD_DIM), lambda b, hh: (b, 0, 2 * h + hh))],
        out_specs=pl.BlockSpec((None, seq, HEAD_DIM), lambda b, hh: (b, 0, hh)),
        scratch_shapes=[pltpu.VMEM((seq, 2 * HEAD_DIM), BF16),
                        pltpu.VMEM((HEAD_DIM, seq), BF16),
                        pltpu.VMEM((2, 2 * MOBA_BLOCK, 2 * MOBA_BLOCK), F32),
                        pltpu.VMEM((LANES, seq), BF16)],
        compiler_params=_cparams(("arbitrary", "arbitrary")),
        name="moba_attention",
    )(qkv, qkv, qkv)


def _lru_kernel(x_ref, y_ref, cw_ref, cb_ref, w_ref, bias_ref, lam_ref, o_ref,
                win_ref, a_ref, b_ref, h_ref, *, ts):
    t_idx = pl.program_id(2)

    @pl.when(t_idx == 0)
    def _():
        win_ref[0:SUBLANES, :] = jnp.zeros((SUBLANES, win_ref.shape[1]), F32)
        h_ref[...] = jnp.zeros_like(h_ref)

    win_ref[SUBLANES:, :] = x_ref[...]
    xc = cb_ref[...] + cw_ref[CONV_WIDTH - 1:CONV_WIDTH, :] * x_ref[...]
    for k in range(CONV_WIDTH - 1):
        shift = CONV_WIDTH - 1 - k
        xc = xc + cw_ref[k:k + 1, :] * win_ref[SUBLANES - shift:SUBLANES - shift + ts, :]
    win_ref[0:SUBLANES, :] = win_ref[ts:ts + SUBLANES, :]

    gates = jnp.dot(xc.astype(BF16), w_ref[...], preferred_element_type=F32) + bias_ref[...]
    cblk = xc.shape[1]
    rt = _sigmoid(gates[:, :cblk])
    it = _sigmoid(gates[:, cblk:])
    neg_lam = -lam_ref[...]
    softplus = jnp.maximum(neg_lam, 0.0) + jnp.log1p(jnp.exp(-jnp.abs(neg_lam)))
    log_a = (-LRU_C) * rt * softplus
    a = jnp.exp(log_a)
    a_ref[...] = a
    b_ref[...] = jnp.sqrt(-jnp.tanh(log_a) * (a * a + 1.0)) * (it * xc)

    row = lax.broadcasted_iota(jnp.int32, (SUBLANES, cblk), 0)

    def group(gi, h_in):
        r0 = pl.multiple_of(gi * SUBLANES, SUBLANES)
        a = a_ref[pl.ds(r0, SUBLANES), :]
        b = b_ref[pl.ds(r0, SUBLANES), :]
        for d in (1, 2, 4):
            keep = row >= d
            a_sh = jnp.where(keep, pltpu.roll(a, d, axis=0), 1.0)
            b_sh = jnp.where(keep, pltpu.roll(b, d, axis=0), 0.0)
            b = a * b_sh + b
            a = a * a_sh
        hs = b + a * h_in
        o_ref[pl.ds(r0, SUBLANES), :] = (hs * y_ref[pl.ds(r0, SUBLANES), :]).astype(o_ref.dtype)
        return jnp.broadcast_to(hs[SUBLANES - 1:SUBLANES, :], (SUBLANES, cblk))

    h_ref[...] = lax.fori_loop(0, ts // SUBLANES, group, h_ref[...], unroll=4)


def _lru(xr, yr, conv_w, conv_b, wa, ba, wx, bx, lam, bsz, seq, ts=2048):
    r = xr.shape[-1]
    ts = min(ts, seq)
    nblk = LRU_BLOCKS
    cblk = r // nblk
    w_cat = jnp.concatenate([wa, wx], axis=-1).astype(BF16)
    b_cat = jnp.concatenate([ba.reshape(nblk, 1, cblk), bx.reshape(nblk, 1, cblk)], axis=-1).astype(F32)
    kern = functools.partial(_lru_kernel, ts=ts)
    chan = lambda b, n, t: (0, n)
    return pl.pallas_call(
        kern,
        out_shape=jax.ShapeDtypeStruct((bsz, seq, r), BF16),
        grid=(bsz, nblk, seq // ts),
        in_specs=[pl.BlockSpec((None, ts, cblk), lambda b, n, t: (b, t, n)),
                  pl.BlockSpec((None, ts, cblk), lambda b, n, t: (b, t, n)),
                  pl.BlockSpec((CONV_WIDTH, cblk), chan),
                  pl.BlockSpec((1, cblk), chan),
                  pl.BlockSpec((None, cblk, 2 * cblk), lambda b, n, t: (n, 0, 0)),
                  pl.BlockSpec((None, 1, 2 * cblk), lambda b, n, t: (n, 0, 0)),
                  pl.BlockSpec((1, cblk), chan)],
        out_specs=pl.BlockSpec((None, ts, cblk), lambda b, n, t: (b, t, n)),
        scratch_shapes=[pltpu.VMEM((ts + SUBLANES, cblk), F32),
                        pltpu.VMEM((ts, cblk), F32),
                        pltpu.VMEM((ts, cblk), F32),
                        pltpu.VMEM((SUBLANES, cblk), F32)],
        compiler_params=_cparams(("parallel", "parallel", "arbitrary")),
        name="conv_rglru",
    )(xr, yr, conv_w.astype(F32), conv_b.reshape(1, r).astype(F32), w_cat, b_cat,
      lam.reshape(1, r).astype(F32))


def _merge_kernel(o_ref, hy_ref, wa_ref, wr_ref, ga_ref, gr_ref, out_ref):
    ya = jnp.dot(o_ref[...], wa_ref[...], preferred_element_type=F32)
    yr = jnp.dot(hy_ref[...], wr_ref[...], preferred_element_type=F32)
    out_ref[...] = (ga_ref[...] * ya + gr_ref[...] * yr).astype(out_ref.dtype)


def _merge(o, hy, wa, wr, gates, tm=1024, tn=512):
    t, k = o.shape
    kr = hy.shape[1]
    n = wa.shape[1]
    nj = n // tn
    return pl.pallas_call(
        _merge_kernel,
        out_shape=jax.ShapeDtypeStruct((t, n), BF16),
        grid=(t // tm, nj),
        in_specs=[pl.BlockSpec((tm, k), lambda i, j: (i, 0)),
                  pl.BlockSpec((tm, kr), lambda i, j: (i, 0)),
                  pl.BlockSpec((k, tn), lambda i, j: (0, j)),
                  pl.BlockSpec((kr, tn), lambda i, j: (0, j)),
                  pl.BlockSpec((tm, tn), lambda i, j: (i, j)),
                  pl.BlockSpec((tm, tn), lambda i, j: (i, nj + j))],
        out_specs=pl.BlockSpec((tm, tn), lambda i, j: (i, j)),
        compiler_params=_cparams(("parallel", "parallel")),
        name="branch_merge",
    )(o, hy, wa, wr, gates, gates)


def _pack_halves(x):
    bits = pltpu.bitcast(x.astype(F32), jnp.uint32)
    half = x.shape[1] // 2
    return (bits[:, :half] >> 16) | bits[:, half:]


def _unpack_halves(w):
    lo = pltpu.bitcast(w << 16, F32).astype(BF16)
    hi = pltpu.bitcast(w & jnp.uint32(0xFFFF0000), F32).astype(BF16)
    return jnp.concatenate([lo, hi], axis=1)


def _outproj_kernel(m_ref, wo_ref, x_ref, g_ref, rw_ref, rb_ref, x1_ref, hf_ref, lg_ref):
    x1 = x_ref[...] + jnp.dot(m_ref[...], wo_ref[...], preferred_element_type=F32)
    x1_ref[...] = x1
    ms = jnp.mean(x1 * x1, axis=-1, keepdims=True)
    hf = (x1 * lax.rsqrt(ms + RMS_EPS) * g_ref[...]).astype(BF16)
    hf_ref[...] = _pack_halves(hf)
    lg_ref[...] = jnp.dot(hf, rw_ref[...], preferred_element_type=F32) + rb_ref[...]


def _outproj(merged, wo, x, g, rw, rb, tm=512):
    t, d = x.shape
    ne = rw.shape[1]
    rw_p = jnp.zeros((d, LANES), BF16).at[:, :ne].set(rw.astype(BF16))
    rb_p = jnp.zeros((1, LANES), F32).at[0, :ne].set(rb.astype(F32))
    row = lambda i: (i, 0)
    fixed = lambda i: (0, 0)
    return pl.pallas_call(
        _outproj_kernel,
        out_shape=(jax.ShapeDtypeStruct((t, d), F32),
                   jax.ShapeDtypeStruct((t, d // 2), jnp.uint32),
                   jax.ShapeDtypeStruct((t, LANES), F32)),
        grid=(t // tm,),
        in_specs=[pl.BlockSpec((tm, d), row),
                  pl.BlockSpec((d, d), fixed),
                  pl.BlockSpec((tm, d), row),
                  pl.BlockSpec((1, d), fixed),
                  pl.BlockSpec((d, LANES), fixed),
                  pl.BlockSpec((1, LANES), fixed)],
        out_specs=(pl.BlockSpec((tm, d), row),
                   pl.BlockSpec((tm, d // 2), row),
                   pl.BlockSpec((tm, LANES), row)),
        compiler_params=_cparams(("parallel",)),
        name="outproj_norm_router",
    )(merged, wo, x, g.reshape(1, d).astype(F32), rw_p, rb_p)


def _expert_changed(be_ref, r):
    return jnp.logical_or(r == 0, be_ref[r] != be_ref[jnp.maximum(r - 1, 0)])


def _expert_up_kernel(be_ref, nu_ref, xs_ref, wg_ref, bg_ref, wu_ref, bu_ref, act_ref,
                      wgb_ref, wub_ref):
    r = pl.program_id(1)
    used = r < nu_ref[0]

    @pl.when(jnp.logical_and(used, _expert_changed(be_ref, r)))
    def _():
        wgb_ref[...] = wg_ref[...].astype(BF16)
        wub_ref[...] = wu_ref[...].astype(BF16)

    @pl.when(used)
    def _():
        xs = _unpack_halves(xs_ref[...])
        glu = jnp.minimum(jnp.dot(xs, wgb_ref[...], preferred_element_type=F32) + bg_ref[...],
                          SWIGLU_LIMIT)
        lin = jnp.clip(jnp.dot(xs, wub_ref[...], preferred_element_type=F32) + bu_ref[...],
                       -SWIGLU_LIMIT, SWIGLU_LIMIT)
        act_ref[...] = (glu * _sigmoid(SWIGLU_ALPHA * glu) * (lin + 1.0)).astype(act_ref.dtype)

    @pl.when(jnp.logical_not(used))
    def _():
        act_ref[...] = jnp.zeros_like(act_ref)


def _expert_down_kernel(be_ref, nu_ref, act_ref, wd_ref, bd_ref, y_ref, wdb_ref):
    r = pl.program_id(1)
    used = r < nu_ref[0]

    @pl.when(jnp.logical_and(used, _expert_changed(be_ref, r)))
    def _():
        wdb_ref[...] = wd_ref[...].astype(BF16)

    @pl.when(used)
    def _():
        y_ref[...] = jnp.dot(act_ref[...], wdb_ref[...], preferred_element_type=F32) + bd_ref[...]

    @pl.when(jnp.logical_not(used))
    def _():
        y_ref[...] = jnp.zeros_like(y_ref)


def _experts(xs, block_expert, n_used, wg, bg, wu, bu, wd, bd, tf=1024, tn=2048):
    p = xs.shape[0]
    ne, d, dff = wg.shape
    tf, tn = min(tf, dff), min(tn, d)
    nblk = p // MOE_BLOCK

    def rr(r, nu):
        return jnp.minimum(r, nu[0] - 1)

    wspec = lambda k, t: pl.BlockSpec((None, k, t), lambda c, r, be, nu: (be[rr(r, nu)], 0, c))
    bspec = lambda t: pl.BlockSpec((None, 1, t), lambda c, r, be, nu: (be[rr(r, nu)], 0, c))
    rows = lambda k: pl.BlockSpec((MOE_BLOCK, k), lambda c, r, be, nu: (rr(r, nu), 0))

    act = pl.pallas_call(
        _expert_up_kernel,
        out_shape=jax.ShapeDtypeStruct((p, dff), BF16),
        grid_spec=pltpu.PrefetchScalarGridSpec(
            num_scalar_prefetch=2,
            grid=(dff // tf, nblk),
            in_specs=[rows(d // 2), wspec(d, tf), bspec(tf), wspec(d, tf), bspec(tf)],
            out_specs=pl.BlockSpec((MOE_BLOCK, tf), lambda c, r, be, nu: (r, c)),
            scratch_shapes=[pltpu.VMEM((d, tf), BF16), pltpu.VMEM((d, tf), BF16)]),
        compiler_params=_cparams(("arbitrary", "arbitrary")),
        name="moe_up",
    )(block_expert, n_used, xs, wg, bg.reshape(ne, 1, dff).astype(F32), wu,
      bu.reshape(ne, 1, dff).astype(F32))

    return pl.pallas_call(
        _expert_down_kernel,
        out_shape=jax.ShapeDtypeStruct((p, d), F32),
        grid_spec=pltpu.PrefetchScalarGridSpec(
            num_scalar_prefetch=2,
            grid=(d // tn, nblk),
            in_specs=[rows(dff), wspec(dff, tn), bspec(tn)],
            out_specs=pl.BlockSpec((MOE_BLOCK, tn), lambda c, r, be, nu: (r, c)),
            scratch_shapes=[pltpu.VMEM((dff, tn), BF16)]),
        compiler_params=_cparams(("arbitrary", "arbitrary")),
        name="moe_down",
    )(block_expert, n_used, act, wd, bd.reshape(ne, 1, d).astype(F32))


def _combine_kernel(dest_ref, x1_ref, gw_ref, g_ref, y_hbm, o_ref, *scratch, tt, ct, norm):
    nch = tt // ct
    bufs, sem_ref = scratch[:nch], scratch[nch]
    i = pl.program_id(0)
    last = pl.num_programs(0) - 1
    lead = 2
    grp = 2 * SUBLANES

    def issue_rows(tok0, n, slot, asg0):
        for j in range(n):
            for k in range(MOE_TOPK):
                row = dest_ref[asg0 + j * MOE_TOPK + k]
                pltpu.make_async_copy(y_hbm.at[pl.ds(row, 1), :],
                                      bufs[slot].at[k, pl.ds(tok0 + j, 1), :],
                                      sem_ref.at[slot]).start()

    def issue_chunk(step, c):
        asg = (step * tt + c * ct) * MOE_TOPK

        def body(g, _):
            tok0 = pl.multiple_of(g * grp, grp)
            issue_rows(tok0, grp, c, asg + tok0 * MOE_TOPK)
            return 0
        lax.fori_loop(0, ct // grp, body, 0)

    def reduce_group(c, tok0):
        rows = pl.ds(c * ct + tok0, grp)
        acc = x1_ref[rows, :]
        gw = gw_ref[rows, :]
        for k in range(MOE_TOPK):
            acc = acc + gw[:, k:k + 1] * bufs[c][k, pl.ds(tok0, grp), :]
        if norm:
            ms = jnp.mean(acc * acc, axis=-1, keepdims=True)
            acc = acc * lax.rsqrt(ms + RMS_EPS) * g_ref[...]
        o_ref[rows, :] = acc

    @pl.when(i == 0)
    def _():
        for c in range(lead):
            issue_chunk(0, c)

    for c in range(nch):
        for k in range(MOE_TOPK):
            pltpu.make_async_copy(y_hbm.at[pl.ds(0, ct), :], bufs[c].at[k], sem_ref.at[c]).wait()
        c_next = (c + lead) % nch
        step_next = i + (c + lead) // nch

        def reduce_and_issue(g, _, c=c, c_next=c_next, step_next=step_next):
            tok0 = pl.multiple_of(g * grp, grp)
            issue_rows(tok0, grp, c_next, ((step_next * tt + c_next * ct) + tok0) * MOE_TOPK)
            reduce_group(c, tok0)
            return 0

        def reduce_only(g, _, c=c):
            reduce_group(c, pl.multiple_of(g * grp, grp))
            return 0

        if c + lead < nch:
            lax.fori_loop(0, ct // grp, reduce_and_issue, 0)
        else:
            @pl.when(i < last)
            def _(f=reduce_and_issue):
                lax.fori_loop(0, ct // grp, f, 0)

            @pl.when(i == last)
            def _(f=reduce_only):
                lax.fori_loop(0, ct // grp, f, 0)


def _combine(x1, y, dest, gate_w, g, norm, tt=256, ct=64):
    t, d = x1.shape
    nch = tt // ct
    kern = functools.partial(_combine_kernel, tt=tt, ct=ct, norm=norm)
    return pl.pallas_call(
        kern,
        out_shape=jax.ShapeDtypeStruct((t, d), F32),
        grid_spec=pltpu.PrefetchScalarGridSpec(
            num_scalar_prefetch=1,
            grid=(t // tt,),
            in_specs=[pl.BlockSpec((tt, d), lambda i, dref: (i, 0)),
                      pl.BlockSpec((tt, MOE_TOPK), lambda i, dref: (i, 0)),
                      pl.BlockSpec((1, d), lambda i, dref: (0, 0)),
                      pl.BlockSpec(memory_space=pl.ANY)],
            out_specs=pl.BlockSpec((tt, d), lambda i, dref: (i, 0)),
            scratch_shapes=[pltpu.VMEM((MOE_TOPK, ct, d), F32) for _ in range(nch)]
            + [pltpu.SemaphoreType.DMA((nch,))]),
        compiler_params=_cparams(("arbitrary",)),
        name="moe_combine",
    )(dest.reshape(t * MOE_TOPK), x1, gate_w, g.reshape(1, d).astype(F32), y)


def _route_kernel(lg_ref, out_ref, cnt_ref, base_ref, *, n_experts, tm):
    @pl.when(pl.program_id(0) == 0)
    def _():
        base_ref[...] = jnp.zeros_like(base_ref)

    lane = lax.broadcasted_iota(jnp.int32, (tm, LANES), 1)
    lg = jnp.where(lane < n_experts, lg_ref[...], -jnp.inf)
    vals, ids = [], []
    onehot = jnp.zeros((tm, LANES), jnp.bool_)
    for _k in range(MOE_TOPK):
        mx = jnp.max(lg, axis=1, keepdims=True)
        first = jnp.min(jnp.where(lg == mx, lane, LANES), axis=1, keepdims=True)
        pick = lane == first
        onehot = jnp.logical_or(onehot, pick)
        lg = jnp.where(pick, -jnp.inf, lg)
        vals.append(mx)
        ids.append(first)
    exps = [jnp.exp(v - vals[0]) for v in vals]
    denom = exps[0] + exps[1] + exps[2] + exps[3]

    oh = jnp.where(onehot, 1.0, 0.0)
    r_i = lax.broadcasted_iota(jnp.int32, (tm, tm), 0)
    c_i = lax.broadcasted_iota(jnp.int32, (tm, tm), 1)
    earlier = jnp.where(r_i > c_i, 1.0, 0.0).astype(BF16)
    before = base_ref[...] + jnp.dot(earlier, oh.astype(BF16), preferred_element_type=F32)
    base_ref[...] = base_ref[...] + jnp.sum(oh, axis=0, keepdims=True)
    cnt_ref[...] = base_ref[...]

    out = jnp.zeros((tm, LANES), F32)
    for k in range(MOE_TOPK):
        rank = jnp.sum(jnp.where(lane == ids[k], before, 0.0), axis=1, keepdims=True)
        out = jnp.where(lane == k, exps[k] / denom, out)
        out = jnp.where(lane == MOE_TOPK + k, ids[k].astype(F32), out)
        out = jnp.where(lane == 2 * MOE_TOPK + k, rank, out)
    out_ref[...] = out


def _route(logits, n_experts, tm=512):
    t = logits.shape[0]
    kern = functools.partial(_route_kernel, n_experts=n_experts, tm=tm)
    packed, counts = pl.pallas_call(
        kern,
        out_shape=(jax.ShapeDtypeStruct((t, LANES), F32), jax.ShapeDtypeStruct((1, LANES), F32)),
        grid=(t // tm,),
        in_specs=[pl.BlockSpec((tm, LANES), lambda i: (i, 0))],
        out_specs=(pl.BlockSpec((tm, LANES), lambda i: (i, 0)),
                   pl.BlockSpec((1, LANES), lambda i: (0, 0))),
        scratch_shapes=[pltpu.VMEM((1, LANES), F32)],
        compiler_params=_cparams(("arbitrary",)),
        name="moe_route",
    )(logits)
    blk = MOE_BLOCK
    n_asg = t * MOE_TOPK
    p = (-(-n_asg // blk)) * blk + n_experts * blk
    cnt = counts[0, :n_experts].astype(jnp.int32)
    padded = (cnt + blk - 1) // blk * blk
    pad_end = jnp.cumsum(padded)
    pad_start = pad_end - padded
    row_starts = jnp.arange(p // blk, dtype=jnp.int32) * blk
    block_expert = jnp.minimum(jnp.sum(pad_end[None, :] <= row_starts[:, None], axis=1),
                               n_experts - 1).astype(jnp.int32)
    n_used = (pad_end[-1] // blk).astype(jnp.int32).reshape(1)
    gate_w = packed[:, :MOE_TOPK]
    ids = packed[:, MOE_TOPK:2 * MOE_TOPK].astype(jnp.int32)
    rank = packed[:, 2 * MOE_TOPK:3 * MOE_TOPK].astype(jnp.int32)
    sel = ids[:, :, None] == jnp.arange(n_experts, dtype=jnp.int32)[None, None, :]
    dest = rank + jnp.sum(jnp.where(sel, pad_start[None, None, :], 0), axis=-1)
    fill_start = jnp.minimum((pad_start + cnt) // SUBLANES * SUBLANES, p - blk).astype(jnp.int32)
    return gate_w, dest, block_expert, n_used, fill_start, p


def _dispatch_kernel(dest_ref, fill_ref, nu_ref, hf_ref, xs_hbm, zero_ref, sem_ref,
                     *, tt, n_experts, nblk):
    i = pl.program_id(0)

    def zero_fill(start):
        return pltpu.make_async_copy(zero_ref, xs_hbm.at[pl.ds(start, MOE_BLOCK), :],
                                     sem_ref.at[0])

    @pl.when(i == 0)
    def _():
        zero_ref[...] = jnp.zeros_like(zero_ref)
        for e in range(n_experts):
            fill = zero_fill(pl.multiple_of(fill_ref[e], SUBLANES))
            fill.start()
            fill.wait()

        def fill_block(r, _):
            fill = zero_fill(pl.multiple_of(r * MOE_BLOCK, MOE_BLOCK))
            fill.start()
            fill.wait()
            return 0

        lax.fori_loop(nu_ref[0], nblk, fill_block, 0)

    base = i * tt

    def body(tok, _):
        for k in range(MOE_TOPK):
            row = dest_ref[(base + tok) * MOE_TOPK + k]
            pltpu.make_async_copy(hf_ref.at[pl.ds(tok, 1), :],
                                  xs_hbm.at[pl.ds(row, 1), :], sem_ref.at[1]).start()
        return 0

    lax.fori_loop(0, tt, body, 0, unroll=2)
    for k in range(MOE_TOPK):
        pltpu.make_async_copy(hf_ref, xs_hbm.at[pl.ds(0, tt), :], sem_ref.at[1]).wait()


def _dispatch(hfp, dest, fill_start, n_used, p, n_experts, tt=512):
    t, w = hfp.shape
    kern = functools.partial(_dispatch_kernel, tt=tt, n_experts=n_experts, nblk=p // MOE_BLOCK)
    return pl.pallas_call(
        kern,
        out_shape=jax.ShapeDtypeStruct((p, w), hfp.dtype),
        grid_spec=pltpu.PrefetchScalarGridSpec(
            num_scalar_prefetch=3,
            grid=(t // tt,),
            in_specs=[pl.BlockSpec((tt, w), lambda i, dref, fref, nref: (i, 0))],
            out_specs=pl.BlockSpec(memory_space=pl.ANY),
            scratch_shapes=[pltpu.VMEM((MOE_BLOCK, w), hfp.dtype),
                            pltpu.SemaphoreType.DMA((2,))]),
        compiler_params=_cparams(("arbitrary",)),
        name="moe_dispatch",
    )(dest.reshape(t * MOE_TOPK), fill_start, n_used, hfp)


def _layer(x2, bsz, seq, norm_mix_g, w_in, conv_w, conv_b, lru_wa, lru_ba, lru_wx, lru_bx,
           lru_lambda, w_attn_branch, w_rnn_branch, w_out, norm_ffn_g, router_w, router_b,
           w_gate, b_gate, w_up, b_up, w_down, b_down):
    t, d = x2.shape
    aw = ATTN_HEADS * HEAD_DIM
    r = lru_lambda.shape[0]
    ne = router_w.shape[1]

    hm = _rmsnorm(x2, norm_mix_g, BF16)
    w_in_b = w_in.astype(BF16)
    qkv = _proj(hm, w_in_b, 0, 3 * aw, BF16, "scale_head",
                n_scaled_tiles=aw // 1024, scale=HEAD_DIM ** -0.5 * LOG2E)
    xr = _proj(hm, w_in_b, 3 * aw, r, F32)
    yr = _proj(hm, w_in_b, 3 * aw + r, r, F32, "gelu")
    gates = _proj(hm, w_in_b, 3 * aw + 2 * r, 2 * d, F32, "sigmoid")

    o = _moba(qkv.reshape(bsz, seq, 3 * aw), bsz, seq).reshape(t, aw)
    hy = _lru(xr.reshape(bsz, seq, r), yr.reshape(bsz, seq, r), conv_w, conv_b,
              lru_wa, lru_ba, lru_wx, lru_bx, lru_lambda, bsz, seq).reshape(t, r)

    merged = _merge(o, hy, w_attn_branch.astype(BF16), w_rnn_branch.astype(BF16), gates)
    x1, hf, logits = _outproj(merged, w_out.astype(BF16), x2, norm_ffn_g, router_w, router_b)

    gate_w, dest, block_expert, n_used, fill_start, p = _route(logits, ne)
    xs = _dispatch(hf, dest, fill_start, n_used, p, ne)
    y = _experts(xs, block_expert, n_used, w_gate, b_gate, w_up, b_up, w_down, b_down)
    return x1, y, dest, gate_w


def kernel(x, norm_mix_g, w_in, conv_w, conv_b, lru_wa, lru_ba, lru_wx, lru_bx, lru_lambda, w_attn_branch, w_rnn_branch, w_out, norm_ffn_g, router_w, router_b, w_gate, b_gate, w_up, b_up, w_down, b_down, norm_final_g):
    bsz, seq, d = x.shape
    depth = w_in.shape[0]
    assert seq % (2 * MOBA_BLOCK) == 0 and d % LANES == 0
    x2 = x.reshape(bsz * seq, d)
    for l in range(depth):
        x1, y, dest, gate_w = _layer(
            x2, bsz, seq, norm_mix_g[l], w_in[l], conv_w[l], conv_b[l], lru_wa[l], lru_ba[l],
            lru_wx[l], lru_bx[l], lru_lambda[l], w_attn_branch[l], w_rnn_branch[l], w_out[l],
            norm_ffn_g[l], router_w[l], router_b[l], w_gate[l], b_gate[l], w_up[l], b_up[l],
            w_down[l], b_down[l])
        last = l + 1 == depth
        x2 = _combine(x1, y, dest, gate_w, norm_final_g, norm=last)
    return x2.reshape(bsz, seq, d)
```

```python
import functools

import jax
import jax.numpy as jnp
from jax import lax
from jax.experimental import pallas as pl
from jax.experimental.pallas import tpu as pltpu

ATTN_HEADS = 16
HEAD_DIM = 128
MOBA_BLOCK = 256
MOBA_TOPK = 3
LRU_BLOCKS = 8
CONV_WIDTH = 4
LRU_C = 8.0
MOE_TOPK = 4
SWIGLU_ALPHA = 1.702
SWIGLU_LIMIT = 7.0
MOE_BLOCK = 512
RMS_EPS = 1e-6

LANES = 128
SUBLANES = 8
VMEM_LIMIT = 56 * 1024 * 1024
LOG2E = 1.4426950408889634
NEG_BIG = -1e30

BF16 = jnp.bfloat16
F32 = jnp.float32


def _cparams(sem):
    return pltpu.CompilerParams(dimension_semantics=sem, vmem_limit_bytes=VMEM_LIMIT)


def _sigmoid(x):
    return 1.0 / (1.0 + jnp.exp(-x))


def _gelu_tanh(x):
    c = 0.7978845608028654
    return 0.5 * x * (1.0 + jnp.tanh(c * (x + 0.044715 * (x * x * x))))


def _rmsnorm_kernel(x_ref, g_ref, o_ref):
    x = x_ref[...]
    ms = jnp.mean(x * x, axis=-1, keepdims=True)
    o_ref[...] = (x * lax.rsqrt(ms + RMS_EPS) * g_ref[...]).astype(o_ref.dtype)


def _rmsnorm(x, g, out_dtype, tm=1024):
    t, d = x.shape
    return pl.pallas_call(
        _rmsnorm_kernel,
        out_shape=jax.ShapeDtypeStruct((t, d), out_dtype),
        grid=(t // tm,),
        in_specs=[pl.BlockSpec((tm, d), lambda i: (i, 0)),
                  pl.BlockSpec((1, d), lambda i: (0, 0))],
        out_specs=pl.BlockSpec((tm, d), lambda i: (i, 0)),
        compiler_params=_cparams(("parallel",)),
        name="rmsnorm",
    )(x, g.reshape(1, d).astype(F32))


def _proj_kernel(a_ref, w_ref, o_ref, *, epilogue, n_scaled_tiles, scale):
    acc = jnp.dot(a_ref[...], w_ref[...], preferred_element_type=F32)
    if epilogue == "scale_head":
        s = jnp.where(pl.program_id(1) < n_scaled_tiles, scale, 1.0).astype(F32)
        acc = acc * s
    elif epilogue == "gelu":
        acc = _gelu_tanh(acc)
    elif epilogue == "sigmoid":
        acc = _sigmoid(acc)
    o_ref[...] = acc.astype(o_ref.dtype)


def _proj(a, w, col0, n, out_dtype, epilogue="none", n_scaled_tiles=0, scale=1.0,
          tm=1024, tn=1024):
    t, k = a.shape
    c0 = col0 // tn
    kern = functools.partial(_proj_kernel, epilogue=epilogue,
                             n_scaled_tiles=n_scaled_tiles, scale=scale)
    return pl.pallas_call(
        kern,
        out_shape=jax.ShapeDtypeStruct((t, n), out_dtype),
        grid=(t // tm, n // tn),
        in_specs=[pl.BlockSpec((tm, k), lambda i, j: (i, 0)),
                  pl.BlockSpec((k, tn), lambda i, j: (0, c0 + j))],
        out_specs=pl.BlockSpec((tm, tn), lambda i, j: (i, j)),
        compiler_params=_cparams(("parallel", "parallel")),
        name="proj_" + epilogue,
    )(a, w)


def _moba_kernel(q_ref, k_ref, v_ref, o_ref, kext_ref, vt_ref, s_ref, avg_ref, *, seq):
    blk = MOBA_BLOCK
    qt = 2 * blk
    nsb = seq // qt

    @pl.when(jnp.logical_and(pl.program_id(0) == 0, pl.program_id(1) == 0))
    def _():
        r_blk = lax.broadcasted_iota(jnp.int32, (seq, LANES), 0) // blk
        c_id = lax.broadcasted_iota(jnp.int32, (seq, LANES), 1)
        kext_ref[:, HEAD_DIM:] = jnp.where(r_blk == c_id, 1.0, 0.0).astype(BF16)
        a_row = lax.broadcasted_iota(jnp.int32, (LANES, seq), 0)
        a_col = lax.broadcasted_iota(jnp.int32, (LANES, seq), 1) // blk
        avg_ref[...] = jnp.where(a_row == a_col, 1.0 / blk, 0.0).astype(BF16)

    kext_ref[:, :HEAD_DIM] = k_ref[...]
    vt_ref[...] = v_ref[...].T
    kmean = jnp.dot(avg_ref[...], k_ref[...], preferred_element_type=F32).astype(BF16)

    nrow = -(-(seq // blk) // SUBLANES) * SUBLANES
    brow = lax.broadcasted_iota(jnp.int32, (nrow, qt), 0)
    second = lax.broadcasted_iota(jnp.int32, (nrow, qt), 1) >= blk
    key_i = lax.broadcasted_iota(jnp.int32, (qt, qt), 0)
    qry_i = lax.broadcasted_iota(jnp.int32, (qt, qt), 1)
    same_blk = (key_i >= blk) == (qry_i >= blk)
    causal_self = jnp.logical_and(same_blk, key_i <= qry_i)
    cross = jnp.logical_and(key_i < blk, qry_i >= blk)

    def q_super(a, _):
        row0 = pl.multiple_of(a * qt, qt)
        q = q_ref[pl.ds(row0, qt), :]
        q_t = q.T
        gate = jnp.dot(kmean, q_t, preferred_element_type=F32)[:nrow, :]
        qblk = 2 * a + second.astype(jnp.int32)
        past = brow < qblk
        g = jnp.where(past, gate, -jnp.inf)
        sel = jnp.zeros((nrow, qt), jnp.bool_)
        for _k in range(MOBA_TOPK):
            mx = jnp.max(g, axis=0, keepdims=True)
            first = jnp.min(jnp.where(g == mx, brow, nrow), axis=0, keepdims=True)
            pick = brow == first
            sel = jnp.logical_or(sel, pick)
            g = jnp.where(pick, -jnp.inf, g)
        chosen = jnp.logical_and(sel, past)
        bias_t = jnp.where(chosen, 0.0, NEG_BIG)
        bias_t = jnp.concatenate([bias_t, jnp.zeros((LANES - nrow, qt), F32)], axis=0)
        qext_t = jnp.concatenate([q_t, bias_t.astype(BF16)], axis=0)

        prev_sel = jnp.max(jnp.where(jnp.logical_and(chosen, brow == 2 * a), 1.0, 0.0),
                           axis=0, keepdims=True) > 0.5
        allowed = jnp.logical_or(causal_self, jnp.logical_and(cross, prev_sel))
        s_diag = jnp.where(allowed, jnp.dot(k_ref[pl.ds(row0, qt), :], q_t,
                                            preferred_element_type=F32), NEG_BIG)
        s_ref[0] = s_diag
        mx_diag = jnp.max(s_diag, axis=0, keepdims=True)

        def absorb(m, l, acc, slot, mx, blk_idx):
            c0 = pl.multiple_of(blk_idx * qt, qt)
            m_new = jnp.maximum(m, mx)
            alpha = jnp.exp2(m - m_new)
            p = jnp.exp2(s_ref[slot] - m_new)
            l = alpha * l + jnp.sum(p, axis=0, keepdims=True)
            acc = alpha * acc + jnp.dot(vt_ref[:, pl.ds(c0, qt)], p.astype(BF16),
                                        preferred_element_type=F32)
            return m_new, l, acc

        def kv_step(carry, b, slot):
            m, l, acc, mx_cur, blk_cur = carry
            c_next = pl.multiple_of(b * qt, qt)
            s_next = jnp.dot(kext_ref[pl.ds(c_next, qt), :], qext_t,
                             preferred_element_type=F32)
            s_ref[1 - slot] = s_next
            mx_next = jnp.max(s_next, axis=0, keepdims=True)
            m, l, acc = absorb(m, l, acc, slot, mx_cur, blk_cur)
            return m, l, acc, mx_next, b

        def kv_pair(b0, carry):
            return kv_step(kv_step(carry, b0, 0), b0 + 1, 1)

        def kv_quad(t, carry):
            return kv_pair(4 * t + 2, kv_pair(4 * t, carry))

        init = (jnp.full((1, qt), NEG_BIG, F32), jnp.zeros((1, qt), F32),
                jnp.zeros((HEAD_DIM, qt), F32), mx_diag, a)
        carry = lax.fori_loop(0, a // 4, kv_quad, init)
        carry = lax.cond(a % 4 >= 2, lambda c: kv_pair(a // 4 * 4, c), lambda c: c, carry)

        def odd_tail(carry):
            m, l, acc, mx_last, blk_last = kv_step(carry, a - 1, 0)
            return absorb(m, l, acc, 1, mx_last, blk_last)

        def even_tail(carry):
            m, l, acc, mx_last, blk_last = carry
            return absorb(m, l, acc, 0, mx_last, blk_last)

        m, l, acc = lax.cond(a % 2 == 1, odd_tail, even_tail, carry)
        o_ref[pl.ds(row0, qt), :] = (acc * (1.0 / l)).T.astype(o_ref.dtype)
        return 0

    lax.fori_loop(0, nsb, q_super, 0)


def _moba(qkv, bsz, seq):
    h = ATTN_HEADS
    kern = functools.partial(_moba_kernel, seq=seq)
    return pl.pallas_call(
        kern,
        out_shape=jax.ShapeDtypeStruct((bsz, seq, h * HEAD_DIM), BF16),
        grid=(bsz, h),
        in_specs=[pl.BlockSpec((None, seq, HEAD_DIM), lambda b, hh: (b, 0, hh)),
                  pl.BlockSpec((None, seq, HEAD_DIM), lambda b, hh: (b, 0, h + hh)),
                  pl.BlockSpec((None, seq, HEAD_DIM), lambda b, hh: (b, 0, 2 * h + hh))],
        out_specs=pl.BlockSpec((None, seq, HEAD_DIM), lambda b, hh: (b, 0, hh)),
        scratch_shapes=[pltpu.VMEM((seq, 2 * HEAD_DIM), BF16),
                        pltpu.VMEM((HEAD_DIM, seq), BF16),
                        pltpu.VMEM((2, 2 * MOBA_BLOCK, 2 * MOBA_BLOCK), F32),
                        pltpu.VMEM((LANES, seq), BF16)],
        compiler_params=_cparams(("arbitrary", "arbitrary")),
        name="moba_attention",
    )(qkv, qkv, qkv)


def _lru_kernel(x_ref, y_ref, cw_ref, cb_ref, w_ref, bias_ref, lam_ref, o_ref,
                win_ref, a_ref, b_ref, h_ref, *, ts):
    t_idx = pl.program_id(2)

    @pl.when(t_idx == 0)
    def _():
        win_ref[0:SUBLANES, :] = jnp.zeros((SUBLANES, win_ref.shape[1]), F32)
        h_ref[...] = jnp.zeros_like(h_ref)

    win_ref[SUBLANES:, :] = x_ref[...]
    xc = cb_ref[...] + cw_ref[CONV_WIDTH - 1:CONV_WIDTH, :] * x_ref[...]
    for k in range(CONV_WIDTH - 1):
        shift = CONV_WIDTH - 1 - k
        xc = xc + cw_ref[k:k + 1, :] * win_ref[SUBLANES - shift:SUBLANES - shift + ts, :]
    win_ref[0:SUBLANES, :] = win_ref[ts:ts + SUBLANES, :]

    gates = jnp.dot(xc.astype(BF16), w_ref[...], preferred_element_type=F32) + bias_ref[...]
    cblk = xc.shape[1]
    rt = _sigmoid(gates[:, :cblk])
    it = _sigmoid(gates[:, cblk:])
    neg_lam = -lam_ref[...]
    softplus = jnp.maximum(neg_lam, 0.0) + jnp.log1p(jnp.exp(-jnp.abs(neg_lam)))
    log_a = (-LRU_C) * rt * softplus
    a = jnp.exp(log_a)
    a_ref[...] = a
    b_ref[...] = jnp.sqrt(-jnp.tanh(log_a) * (a * a + 1.0)) * (it * xc)

    row = lax.broadcasted_iota(jnp.int32, (SUBLANES, cblk), 0)

    def group(gi, h_in):
        r0 = pl.multiple_of(gi * SUBLANES, SUBLANES)
        a = a_ref[pl.ds(r0, SUBLANES), :]
        b = b_ref[pl.ds(r0, SUBLANES), :]
        for d in (1, 2, 4):
            keep = row >= d
            a_sh = jnp.where(keep, pltpu.roll(a, d, axis=0), 1.0)
            b_sh = jnp.where(keep, pltpu.roll(b, d, axis=0), 0.0)
            b = a * b_sh + b
            a = a * a_sh
        hs = b + a * h_in
        o_ref[pl.ds(r0, SUBLANES), :] = (hs * y_ref[pl.ds(r0, SUBLANES), :]).astype(o_ref.dtype)
        return jnp.broadcast_to(hs[SUBLANES - 1:SUBLANES, :], (SUBLANES, cblk))

    h_ref[...] = lax.fori_loop(0, ts // SUBLANES, group, h_ref[...], unroll=4)


def _lru(xr, yr, conv_w, conv_b, wa, ba, wx, bx, lam, bsz, seq, ts=2048):
    r = xr.shape[-1]
    ts = min(ts, seq)
    nblk = LRU_BLOCKS
    cblk = r // nblk
    w_cat = jnp.concatenate([wa, wx], axis=-1).astype(BF16)
    b_cat = jnp.concatenate([ba.reshape(nblk, 1, cblk), bx.reshape(nblk, 1, cblk)], axis=-1).astype(F32)
    kern = functools.partial(_lru_kernel, ts=ts)
    chan = lambda b, n, t: (0, n)
    return pl.pallas_call(
        kern,
        out_shape=jax.ShapeDtypeStruct((bsz, seq, r), BF16),
        grid=(bsz, nblk, seq // ts),
        in_specs=[pl.BlockSpec((None, ts, cblk), lambda b, n, t: (b, t, n)),
                  pl.BlockSpec((None, ts, cblk), lambda b, n, t: (b, t, n)),
                  pl.BlockSpec((CONV_WIDTH, cblk), chan),
                  pl.BlockSpec((1, cblk), chan),
                  pl.BlockSpec((None, cblk, 2 * cblk), lambda b, n, t: (n, 0, 0)),
                  pl.BlockSpec((None, 1, 2 * cblk), lambda b, n, t: (n, 0, 0)),
                  pl.BlockSpec((1, cblk), chan)],
        out_specs=pl.BlockSpec((None, ts, cblk), lambda b, n, t: (b, t, n)),
        scratch_shapes=[pltpu.VMEM((ts + SUBLANES, cblk), F32),
                        pltpu.VMEM((ts, cblk), F32),
                        pltpu.VMEM((ts, cblk), F32),
                        pltpu.VMEM((SUBLANES, cblk), F32)],
        compiler_params=_cparams(("parallel", "parallel", "arbitrary")),
        name="conv_rglru",
    )(xr, yr, conv_w.astype(F32), conv_b.reshape(1, r).astype(F32), w_cat, b_cat,
      lam.reshape(1, r).astype(F32))


def _merge_kernel(o_ref, hy_ref, wa_ref, wr_ref, ga_ref, gr_ref, out_ref):
    ya = jnp.dot(o_ref[...], wa_ref[...], preferred_element_type=F32)
    yr = jnp.dot(hy_ref[...], wr_ref[...], preferred_element_type=F32)
    out_ref[...] = (ga_ref[...] * ya + gr_ref[...] * yr).astype(out_ref.dtype)


def _merge(o, hy, wa, wr, gates, tm=1024, tn=512):
    t, k = o.shape
    kr = hy.shape[1]
    n = wa.shape[1]
    nj = n // tn
    return pl.pallas_call(
        _merge_kernel,
        out_shape=jax.ShapeDtypeStruct((t, n), BF16),
        grid=(t // tm, nj),
        in_specs=[pl.BlockSpec((tm, k), lambda i, j: (i, 0)),
                  pl.BlockSpec((tm, kr), lambda i, j: (i, 0)),
                  pl.BlockSpec((k, tn), lambda i, j: (0, j)),
                  pl.BlockSpec((kr, tn), lambda i, j: (0, j)),
                  pl.BlockSpec((tm, tn), lambda i, j: (i, j)),
                  pl.BlockSpec((tm, tn), lambda i, j: (i, nj + j))],
        out_specs=pl.BlockSpec((tm, tn), lambda i, j: (i, j)),
        compiler_params=_cparams(("parallel", "parallel")),
        name="branch_merge",
    )(o, hy, wa, wr, gates, gates)


def _pack_halves(x):
    bits = pltpu.bitcast(x.astype(F32), jnp.uint32)
    half = x.shape[1] // 2
    return (bits[:, :half] >> 16) | bits[:, half:]


def _unpack_halves(w):
    lo = pltpu.bitcast(w << 16, F32).astype(BF16)
    hi = pltpu.bitcast(w & jnp.uint32(0xFFFF0000), F32).astype(BF16)
    return jnp.concatenate([lo, hi], axis=1)


def _outproj_kernel(m_ref, wo_ref, x_ref, g_ref, rw_ref, rb_ref, x1_ref, hf_ref, lg_ref):
    x1 = x_ref[...] + jnp.dot(m_ref[...], wo_ref[...], preferred_element_type=F32)
    x1_ref[...] = x1
    ms = jnp.mean(x1 * x1, axis=-1, keepdims=True)
    hf = (x1 * lax.rsqrt(ms + RMS_EPS) * g_ref[...]).astype(BF16)
    hf_ref[...] = _pack_halves(hf)
    lg_ref[...] = jnp.dot(hf, rw_ref[...], preferred_element_type=F32) + rb_ref[...]


def _outproj(merged, wo, x, g, rw, rb, tm=512):
    t, d = x.shape
    ne = rw.shape[1]
    rw_p = jnp.zeros((d, LANES), BF16).at[:, :ne].set(rw.astype(BF16))
    rb_p = jnp.zeros((1, LANES), F32).at[0, :ne].set(rb.astype(F32))
    row = lambda i: (i, 0)
    fixed = lambda i: (0, 0)
    return pl.pallas_call(
        _outproj_kernel,
        out_shape=(jax.ShapeDtypeStruct((t, d), F32),
                   jax.ShapeDtypeStruct((t, d // 2), jnp.uint32),
                   jax.ShapeDtypeStruct((t, LANES), F32)),
        grid=(t // tm,),
        in_specs=[pl.BlockSpec((tm, d), row),
                  pl.BlockSpec((d, d), fixed),
                  pl.BlockSpec((tm, d), row),
                  pl.BlockSpec((1, d), fixed),
                  pl.BlockSpec((d, LANES), fixed),
                  pl.BlockSpec((1, LANES), fixed)],
        out_specs=(pl.BlockSpec((tm, d), row),
                   pl.BlockSpec((tm, d // 2), row),
                   pl.BlockSpec((tm, LANES), row)),
        compiler_params=_cparams(("parallel",)),
        name="outproj_norm_router",
    )(merged, wo, x, g.reshape(1, d).astype(F32), rw_p, rb_p)


def _expert_changed(be_ref, r):
    return jnp.logical_or(r == 0, be_ref[r] != be_ref[jnp.maximum(r - 1, 0)])


def _expert_up_kernel(be_ref, nu_ref, xs_ref, wg_ref, bg_ref, wu_ref, bu_ref, act_ref,
                      wgb_ref, wub_ref):
    r = pl.program_id(1)
    used = r < nu_ref[0]

    @pl.when(jnp.logical_and(used, _expert_changed(be_ref, r)))
    def _():
        wgb_ref[...] = wg_ref[...].astype(BF16)
        wub_ref[...] = wu_ref[...].astype(BF16)

    @pl.when(used)
    def _():
        xs = _unpack_halves(xs_ref[...])
        glu = jnp.minimum(jnp.dot(xs, wgb_ref[...], preferred_element_type=F32) + bg_ref[...],
                          SWIGLU_LIMIT)
        lin = jnp.clip(jnp.dot(xs, wub_ref[...], preferred_element_type=F32) + bu_ref[...],
                       -SWIGLU_LIMIT, SWIGLU_LIMIT)
        act_ref[...] = (glu * _sigmoid(SWIGLU_ALPHA * glu) * (lin + 1.0)).astype(act_ref.dtype)

    @pl.when(jnp.logical_not(used))
    def _():
        act_ref[...] = jnp.zeros_like(act_ref)


def _expert_down_kernel(be_ref, nu_ref, act_ref, wd_ref, bd_ref, y_ref, wdb_ref):
    r = pl.program_id(1)
    used = r < nu_ref[0]

    @pl.when(jnp.logical_and(used, _expert_changed(be_ref, r)))
    def _():
        wdb_ref[...] = wd_ref[...].astype(BF16)

    @pl.when(used)
    def _():
        y_ref[...] = jnp.dot(act_ref[...], wdb_ref[...], preferred_element_type=F32) + bd_ref[...]

    @pl.when(jnp.logical_not(used))
    def _():
        y_ref[...] = jnp.zeros_like(y_ref)


def _experts(xs, block_expert, n_used, wg, bg, wu, bu, wd, bd, tf=1024, tn=2048):
    p = xs.shape[0]
    ne, d, dff = wg.shape
    tf, tn = min(tf, dff), min(tn, d)
    nblk = p // MOE_BLOCK

    def rr(r, nu):
        return jnp.minimum(r, nu[0] - 1)

    wspec = lambda k, t: pl.BlockSpec((None, k, t), lambda c, r, be, nu: (be[rr(r, nu)], 0, c))
    bspec = lambda t: pl.BlockSpec((None, 1, t), lambda c, r, be, nu: (be[rr(r, nu)], 0, c))
    rows = lambda k: pl.BlockSpec((MOE_BLOCK, k), lambda c, r, be, nu: (rr(r, nu), 0))

    act = pl.pallas_call(
        _expert_up_kernel,
        out_shape=jax.ShapeDtypeStruct((p, dff), BF16),
        grid_spec=pltpu.PrefetchScalarGridSpec(
            num_scalar_prefetch=2,
            grid=(dff // tf, nblk),
            in_specs=[rows(d // 2), wspec(d, tf), bspec(tf), wspec(d, tf), bspec(tf)],
            out_specs=pl.BlockSpec((MOE_BLOCK, tf), lambda c, r, be, nu: (r, c)),
            scratch_shapes=[pltpu.VMEM((d, tf), BF16), pltpu.VMEM((d, tf), BF16)]),
        compiler_params=_cparams(("arbitrary", "arbitrary")),
        name="moe_up",
    )(block_expert, n_used, xs, wg, bg.reshape(ne, 1, dff).astype(F32), wu,
      bu.reshape(ne, 1, dff).astype(F32))

    return pl.pallas_call(
        _expert_down_kernel,
        out_shape=jax.ShapeDtypeStruct((p, d), F32),
        grid_spec=pltpu.PrefetchScalarGridSpec(
            num_scalar_prefetch=2,
            grid=(d // tn, nblk),
            in_specs=[rows(dff), wspec(dff, tn), bspec(tn)],
            out_specs=pl.BlockSpec((MOE_BLOCK, tn), lambda c, r, be, nu: (r, c)),
            scratch_shapes=[pltpu.VMEM((dff, tn), BF16)]),
        compiler_params=_cparams(("arbitrary", "arbitrary")),
        name="moe_down",
    )(block_expert, n_used, act, wd, bd.reshape(ne, 1, d).astype(F32))


def _combine_kernel(dest_ref, x1_ref, gw_ref, g_ref, y_hbm, o_ref, *scratch, tt, ct, norm):
    nch = tt // ct
    bufs, sem_ref = scratch[:nch], scratch[nch]
    i = pl.program_id(0)
    last = pl.num_programs(0) - 1
    lead = 2
    grp = 2 * SUBLANES

    def issue_rows(tok0, n, slot, asg0):
        for j in range(n):
            for k in range(MOE_TOPK):
                row = dest_ref[asg0 + j * MOE_TOPK + k]
                pltpu.make_async_copy(y_hbm.at[pl.ds(row, 1), :],
                                      bufs[slot].at[k, pl.ds(tok0 + j, 1), :],
                                      sem_ref.at[slot]).start()

    def issue_chunk(step, c):
        asg = (step * tt + c * ct) * MOE_TOPK

        def body(g, _):
            tok0 = pl.multiple_of(g * grp, grp)
            issue_rows(tok0, grp, c, asg + tok0 * MOE_TOPK)
            return 0
        lax.fori_loop(0, ct // grp, body, 0)

    def reduce_group(c, tok0):
        rows = pl.ds(c * ct + tok0, grp)
        acc = x1_ref[rows, :]
        gw = gw_ref[rows, :]
        for k in range(MOE_TOPK):
            acc = acc + gw[:, k:k + 1] * bufs[c][k, pl.ds(tok0, grp), :]
        if norm:
            ms = jnp.mean(acc * acc, axis=-1, keepdims=True)
            acc = acc * lax.rsqrt(ms + RMS_EPS) * g_ref[...]
        o_ref[rows, :] = acc

    @pl.when(i == 0)
    def _():
        for c in range(lead):
            issue_chunk(0, c)

    for c in range(nch):
        for k in range(MOE_TOPK):
            pltpu.make_async_copy(y_hbm.at[pl.ds(0, ct), :], bufs[c].at[k], sem_ref.at[c]).wait()
        c_next = (c + lead) % nch
        step_next = i + (c + lead) // nch

        def reduce_and_issue(g, _, c=c, c_next=c_next, step_next=step_next):
            tok0 = pl.multiple_of(g * grp, grp)
            issue_rows(tok0, grp, c_next, ((step_next * tt + c_next * ct) + tok0) * MOE_TOPK)
            reduce_group(c, tok0)
            return 0

        def reduce_only(g, _, c=c):
            reduce_group(c, pl.multiple_of(g * grp, grp))
            return 0

        if c + lead < nch:
            lax.fori_loop(0, ct // grp, reduce_and_issue, 0)
        else:
            @pl.when(i < last)
            def _(f=reduce_and_issue):
                lax.fori_loop(0, ct // grp, f, 0)

            @pl.when(i == last)
            def _(f=reduce_only):
                lax.fori_loop(0, ct // grp, f, 0)


def _combine(x1, y, dest, gate_w, g, norm, tt=256, ct=64):
    t, d = x1.shape
    nch = tt // ct
    kern = functools.partial(_combine_kernel, tt=tt, ct=ct, norm=norm)
    return pl.pallas_call(
        kern,
        out_shape=jax.ShapeDtypeStruct((t, d), F32),
        grid_spec=pltpu.PrefetchScalarGridSpec(
            num_scalar_prefetch=1,
            grid=(t // tt,),
            in_specs=[pl.BlockSpec((tt, d), lambda i, dref: (i, 0)),
                      pl.BlockSpec((tt, MOE_TOPK), lambda i, dref: (i, 0)),
                      pl.BlockSpec((1, d), lambda i, dref: (0, 0)),
                      pl.BlockSpec(memory_space=pl.ANY)],
            out_specs=pl.BlockSpec((tt, d), lambda i, dref: (i, 0)),
            scratch_shapes=[pltpu.VMEM((MOE_TOPK, ct, d), F32) for _ in range(nch)]
            + [pltpu.SemaphoreType.DMA((nch,))]),
        compiler_params=_cparams(("arbitrary",)),
        name="moe_combine",
    )(dest.reshape(t * MOE_TOPK), x1, gate_w, g.reshape(1, d).astype(F32), y)


def _route_kernel(lg_ref, out_ref, cnt_ref, base_ref, *, n_experts, tm):
    @pl.when(pl.program_id(0) == 0)
    def _():
        base_ref[...] = jnp.zeros_like(base_ref)

    lane = lax.broadcasted_iota(jnp.int32, (tm, LANES), 1)
    lg = jnp.where(lane < n_experts, lg_ref[...], -jnp.inf)
    vals, ids = [], []
    onehot = jnp.zeros((tm, LANES), jnp.bool_)
    for _k in range(MOE_TOPK):
        mx = jnp.max(lg, axis=1, keepdims=True)
        first = jnp.min(jnp.where(lg == mx, lane, LANES), axis=1, keepdims=True)
        pick = lane == first
        onehot = jnp.logical_or(onehot, pick)
        lg = jnp.where(pick, -jnp.inf, lg)
        vals.append(mx)
        ids.append(first)
    exps = [jnp.exp(v - vals[0]) for v in vals]
    denom = exps[0] + exps[1] + exps[2] + exps[3]

    oh = jnp.where(onehot, 1.0, 0.0)
    r_i = lax.broadcasted_iota(jnp.int32, (tm, tm), 0)
    c_i = lax.broadcasted_iota(jnp.int32, (tm, tm), 1)
    earlier = jnp.where(r_i > c_i, 1.0, 0.0).astype(BF16)
    before = base_ref[...] + jnp.dot(earlier, oh.astype(BF16), preferred_element_type=F32)
    base_ref[...] = base_ref[...] + jnp.sum(oh, axis=0, keepdims=True)
    cnt_ref[...] = base_ref[...]

    out = jnp.zeros((tm, LANES), F32)
    for k in range(MOE_TOPK):
        rank = jnp.sum(jnp.where(lane == ids[k], before, 0.0), axis=1, keepdims=True)
        out = jnp.where(lane == k, exps[k] / denom, out)
        out = jnp.where(lane == MOE_TOPK + k, ids[k].astype(F32), out)
        out = jnp.where(lane == 2 * MOE_TOPK + k, rank, out)
    out_ref[...] = out


def _route(logits, n_experts, tm=512):
    t = logits.shape[0]
    kern = functools.partial(_route_kernel, n_experts=n_experts, tm=tm)
    packed, counts = pl.pallas_call(
        kern,
        out_shape=(jax.ShapeDtypeStruct((t, LANES), F32), jax.ShapeDtypeStruct((1, LANES), F32)),
        grid=(t // tm,),
        in_specs=[pl.BlockSpec((tm, LANES), lambda i: (i, 0))],
        out_specs=(pl.BlockSpec((tm, LANES), lambda i: (i, 0)),
                   pl.BlockSpec((1, LANES), lambda i: (0, 0))),
        scratch_shapes=[pltpu.VMEM((1, LANES), F32)],
        compiler_params=_cparams(("arbitrary",)),
        name="moe_route",
    )(logits)
    blk = MOE_BLOCK
    n_asg = t * MOE_TOPK
    p = (-(-n_asg // blk)) * blk + n_experts * blk
    cnt = counts[0, :n_experts].astype(jnp.int32)
    padded = (cnt + blk - 1) // blk * blk
    pad_end = jnp.cumsum(padded)
    pad_start = pad_end - padded
    row_starts = jnp.arange(p // blk, dtype=jnp.int32) * blk
    block_expert = jnp.minimum(jnp.sum(pad_end[None, :] <= row_starts[:, None], axis=1),
                               n_experts - 1).astype(jnp.int32)
    n_used = (pad_end[-1] // blk).astype(jnp.int32).reshape(1)
    gate_w = packed[:, :MOE_TOPK]
    ids = packed[:, MOE_TOPK:2 * MOE_TOPK].astype(jnp.int32)
    rank = packed[:, 2 * MOE_TOPK:3 * MOE_TOPK].astype(jnp.int32)
    sel = ids[:, :, None] == jnp.arange(n_experts, dtype=jnp.int32)[None, None, :]
    dest = rank + jnp.sum(jnp.where(sel, pad_start[None, None, :], 0), axis=-1)
    fill_start = jnp.minimum((pad_start + cnt) // SUBLANES * SUBLANES, p - blk).astype(jnp.int32)
    return gate_w, dest, block_expert, n_used, fill_start, p


def _dispatch_kernel(dest_ref, fill_ref, nu_ref, hf_ref, xs_hbm, zero_ref, sem_ref,
                     *, tt, n_experts, nblk):
    i = pl.program_id(0)

    def zero_fill(start):
        return pltpu.make_async_copy(zero_ref, xs_hbm.at[pl.ds(start, MOE_BLOCK), :],
                                     sem_ref.at[0])

    @pl.when(i == 0)
    def _():
        zero_ref[...] = jnp.zeros_like(zero_ref)
        for e in range(n_experts):
            fill = zero_fill(pl.multiple_of(fill_ref[e], SUBLANES))
            fill.start()
            fill.wait()

        def fill_block(r, _):
            fill = zero_fill(pl.multiple_of(r * MOE_BLOCK, MOE_BLOCK))
            fill.start()
            fill.wait()
            return 0

        lax.fori_loop(nu_ref[0], nblk, fill_block, 0)

    base = i * tt

    def body(tok, _):
        for k in range(MOE_TOPK):
            row = dest_ref[(base + tok) * MOE_TOPK + k]
            pltpu.make_async_copy(hf_ref.at[pl.ds(tok, 1), :],
                                  xs_hbm.at[pl.ds(row, 1), :], sem_ref.at[1]).start()
        return 0

    lax.fori_loop(0, tt, body, 0, unroll=2)
    for k in range(MOE_TOPK):
        pltpu.make_async_copy(hf_ref, xs_hbm.at[pl.ds(0, tt), :], sem_ref.at[1]).wait()


def _dispatch(hfp, dest, fill_start, n_used, p, n_experts, tt=2048):
    t, w = hfp.shape
    tt = min(tt, t)
    kern = functools.partial(_dispatch_kernel, tt=tt, n_experts=n_experts, nblk=p // MOE_BLOCK)
    return pl.pallas_call(
        kern,
        out_shape=jax.ShapeDtypeStruct((p, w), hfp.dtype),
        grid_spec=pltpu.PrefetchScalarGridSpec(
            num_scalar_prefetch=3,
            grid=(t // tt,),
            in_specs=[pl.BlockSpec((tt, w), lambda i, dref, fref, nref: (i, 0))],
            out_specs=pl.BlockSpec(memory_space=pl.ANY),
            scratch_shapes=[pltpu.VMEM((MOE_BLOCK, w), hfp.dtype),
                            pltpu.SemaphoreType.DMA((2,))]),
        compiler_params=_cparams(("arbitrary",)),
        name="moe_dispatch",
    )(dest.reshape(t * MOE_TOPK), fill_start, n_used, hfp)


def _layer(x2, bsz, seq, norm_mix_g, w_in, conv_w, conv_b, lru_wa, lru_ba, lru_wx, lru_bx,
           lru_lambda, w_attn_branch, w_rnn_branch, w_out, norm_ffn_g, router_w, router_b,
           w_gate, b_gate, w_up, b_up, w_down, b_down):
    t, d = x2.shape
    aw = ATTN_HEADS * HEAD_DIM
    r = lru_lambda.shape[0]
    ne = router_w.shape[1]

    hm = _rmsnorm(x2, norm_mix_g, BF16)
    w_in_b = w_in.astype(BF16)
    qkv = _proj(hm, w_in_b, 0, 3 * aw, BF16, "scale_head",
                n_scaled_tiles=aw // 1024, scale=HEAD_DIM ** -0.5 * LOG2E)
    xr = _proj(hm, w_in_b, 3 * aw, r, F32)
    yr = _proj(hm, w_in_b, 3 * aw + r, r, F32, "gelu")
    gates = _proj(hm, w_in_b, 3 * aw + 2 * r, 2 * d, F32, "sigmoid")

    o = _moba(qkv.reshape(bsz, seq, 3 * aw), bsz, seq).reshape(t, aw)
    hy = _lru(xr.reshape(bsz, seq, r), yr.reshape(bsz, seq, r), conv_w, conv_b,
              lru_wa, lru_ba, lru_wx, lru_bx, lru_lambda, bsz, seq).reshape(t, r)

    merged = _merge(o, hy, w_attn_branch.astype(BF16), w_rnn_branch.astype(BF16), gates)
    x1, hf, logits = _outproj(merged, w_out.astype(BF16), x2, norm_ffn_g, router_w, router_b)

    gate_w, dest, block_expert, n_used, fill_start, p = _route(logits, ne)
    xs = _dispatch(hf, dest, fill_start, n_used, p, ne)
    y = _experts(xs, block_expert, n_used, w_gate, b_gate, w_up, b_up, w_down, b_down)
    return x1, y, dest, gate_w


def kernel(x, norm_mix_g, w_in, conv_w, conv_b, lru_wa, lru_ba, lru_wx, lru_bx, lru_lambda, w_attn_branch, w_rnn_branch, w_out, norm_ffn_g, router_w, router_b, w_gate, b_gate, w_up, b_up, w_down, b_down, norm_final_g):
    bsz, seq, d = x.shape
    depth = w_in.shape[0]
    assert seq % (2 * MOBA_BLOCK) == 0 and seq // MOBA_BLOCK <= LANES and d % LANES == 0
    x2 = x.reshape(bsz * seq, d)
    for l in range(depth):
        x1, y, dest, gate_w = _layer(
            x2, bsz, seq, norm_mix_g[l], w_in[l], conv_w[l], conv_b[l], lru_wa[l], lru_ba[l],
            lru_wx[l], lru_bx[l], lru_lambda[l], w_attn_branch[l], w_rnn_branch[l], w_out[l],
            norm_ffn_g[l], router_w[l], router_b[l], w_gate[l], b_gate[l], w_up[l], b_up[l],
            w_down[l], b_down[l])
        last = l + 1 == depth
        x2 = _combine(x1, y, dest, gate_w, norm_final_g, norm=last)
    return x2.reshape(bsz, seq, d)
```

```python
import functools
import math

import jax
import jax.numpy as jnp
from jax import lax
from jax.experimental import pallas as pl
from jax.experimental.pallas import tpu as pltpu

ATTN_HEADS = 16
HEAD_DIM = 128
MOBA_BLOCK = 256
MOBA_TOPK = 3
LRU_BLOCKS = 8
CONV_WIDTH = 4
LRU_C = 8.0
MOE_TOPK = 4
SWIGLU_ALPHA = 1.702
SWIGLU_LIMIT = 7.0
MOE_BLOCK = 512
RMS_EPS = 1e-6

LANES = 128
SUBLANES = 8
VMEM_LIMIT = 56 * 1024 * 1024
LOG2E = 1.4426950408889634
NEG_BIG = -1e30

BF16 = jnp.bfloat16
F32 = jnp.float32


def _cparams(sem):
    return pltpu.CompilerParams(dimension_semantics=sem, vmem_limit_bytes=VMEM_LIMIT)


def _sigmoid(x):
    return 1.0 / (1.0 + jnp.exp(-x))


def _gelu_tanh(x):
    c = 0.7978845608028654
    return 0.5 * x * (1.0 + jnp.tanh(c * (x + 0.044715 * (x * x * x))))


def _rmsnorm_kernel(x_ref, g_ref, o_ref):
    x = x_ref[...]
    ms = jnp.mean(x * x, axis=-1, keepdims=True)
    o_ref[...] = (x * lax.rsqrt(ms + RMS_EPS) * g_ref[...]).astype(o_ref.dtype)


def _rmsnorm(x, g, out_dtype, tm=1024):
    t, d = x.shape
    return pl.pallas_call(
        _rmsnorm_kernel,
        out_shape=jax.ShapeDtypeStruct((t, d), out_dtype),
        grid=(t // tm,),
        in_specs=[pl.BlockSpec((tm, d), lambda i: (i, 0)),
                  pl.BlockSpec((1, d), lambda i: (0, 0))],
        out_specs=pl.BlockSpec((tm, d), lambda i: (i, 0)),
        compiler_params=_cparams(("parallel",)),
        name="rmsnorm",
    )(x, g.reshape(1, d).astype(F32))


def _proj_kernel(a_ref, w_ref, o_ref, *, epilogue, n_scaled, scale):
    acc = jnp.dot(a_ref[...], w_ref[...], preferred_element_type=F32)
    if epilogue == "scale_head":
        n_scaled_tiles = n_scaled // o_ref.shape[1]
        s = jnp.where(pl.program_id(1) < n_scaled_tiles, scale, 1.0).astype(F32)
        acc = acc * s
    elif epilogue == "gelu":
        acc = _gelu_tanh(acc)
    elif epilogue == "sigmoid":
        acc = _sigmoid(acc)
    o_ref[...] = acc.astype(o_ref.dtype)


def _proj(a, w, col0, n, out_dtype, epilogue="none", n_scaled=0, scale=1.0,
          tm=1024, tn=2048):
    t, k = a.shape
    tn = math.gcd(tn, col0, n, n_scaled)
    c0 = col0 // tn
    kern = functools.partial(_proj_kernel, epilogue=epilogue, n_scaled=n_scaled, scale=scale)
    return pl.pallas_call(
        kern,
        out_shape=jax.ShapeDtypeStruct((t, n), out_dtype),
        grid=(t // tm, n // tn),
        in_specs=[pl.BlockSpec((tm, k), lambda i, j: (i, 0)),
                  pl.BlockSpec((k, tn), lambda i, j: (0, c0 + j))],
        out_specs=pl.BlockSpec((tm, tn), lambda i, j: (i, j)),
        compiler_params=_cparams(("parallel", "parallel")),
        name="proj_" + epilogue,
    )(a, w)


def _moba_kernel(q_ref, k_ref, v_ref, o_ref, kext_ref, vt_ref, sa_ref, sb_ref, sc_ref, avg_ref,
                 *, seq):
    blk = MOBA_BLOCK
    qt = 2 * blk
    nsb = seq // qt

    @pl.when(jnp.logical_and(pl.program_id(0) == 0, pl.program_id(1) == 0))
    def _():
        r_blk = lax.broadcasted_iota(jnp.int32, (seq, LANES), 0) // blk
        c_id = lax.broadcasted_iota(jnp.int32, (seq, LANES), 1)
        kext_ref[:, HEAD_DIM:] = jnp.where(r_blk == c_id, 1.0, 0.0).astype(BF16)
        a_row = lax.broadcasted_iota(jnp.int32, (LANES, seq), 0)
        a_col = lax.broadcasted_iota(jnp.int32, (LANES, seq), 1) // blk
        avg_ref[...] = jnp.where(a_row == a_col, 1.0 / blk, 0.0).astype(BF16)

    kext_ref[:, :HEAD_DIM] = k_ref[...]
    vt_ref[...] = v_ref[...].T
    kmean = jnp.dot(avg_ref[...], k_ref[...], preferred_element_type=F32).astype(BF16)

    nrow = -(-(seq // blk) // SUBLANES) * SUBLANES
    brow = lax.broadcasted_iota(jnp.int32, (nrow, qt), 0)
    second = lax.broadcasted_iota(jnp.int32, (nrow, qt), 1) >= blk
    key_i = lax.broadcasted_iota(jnp.int32, (qt, qt), 0)
    qry_i = lax.broadcasted_iota(jnp.int32, (qt, qt), 1)
    same_blk = (key_i >= blk) == (qry_i >= blk)
    causal_self = jnp.logical_and(same_blk, key_i <= qry_i)
    cross = jnp.logical_and(key_i < blk, qry_i >= blk)

    def prologue(a, diag_ref):
        row0 = pl.multiple_of(a * qt, qt)
        q = q_ref[pl.ds(row0, qt), :]
        q_t = q.T
        gate = jnp.dot(kmean, q_t, preferred_element_type=F32)[:nrow, :]
        qblk = 2 * a + second.astype(jnp.int32)
        past = brow < qblk
        g = jnp.where(past, gate, -jnp.inf)
        sel = jnp.zeros((nrow, qt), jnp.bool_)
        for _k in range(MOBA_TOPK):
            mx = jnp.max(g, axis=0, keepdims=True)
            first = jnp.min(jnp.where(g == mx, brow, nrow), axis=0, keepdims=True)
            pick = brow == first
            sel = jnp.logical_or(sel, pick)
            g = jnp.where(pick, -jnp.inf, g)
        chosen = jnp.logical_and(sel, past)
        bias_t = jnp.where(chosen, 0.0, NEG_BIG)
        bias_t = jnp.concatenate([bias_t, jnp.zeros((LANES - nrow, qt), F32)], axis=0)
        qext_t = jnp.concatenate([q_t, bias_t.astype(BF16)], axis=0)

        prev_sel = jnp.max(jnp.where(jnp.logical_and(chosen, brow == 2 * a), 1.0, 0.0),
                           axis=0, keepdims=True) > 0.5
        allowed = jnp.logical_or(causal_self, jnp.logical_and(cross, prev_sel))
        s_diag = jnp.where(allowed, jnp.dot(k_ref[pl.ds(row0, qt), :], q_t,
                                            preferred_element_type=F32), NEG_BIG)
        diag_ref[...] = s_diag
        return qext_t, jnp.max(s_diag, axis=0, keepdims=True)

    def attend(a, qext_t, mx_diag, bufs):
        row0 = pl.multiple_of(a * qt, qt)

        def absorb(m, l, acc, slot, mx, blk_idx):
            c0 = pl.multiple_of(blk_idx * qt, qt)
            m_new = jnp.maximum(m, mx)
            alpha = jnp.exp2(m - m_new)
            p = jnp.exp2(bufs[slot][...] - m_new)
            l = alpha * l + jnp.sum(p, axis=0, keepdims=True)
            acc = alpha * acc + jnp.dot(vt_ref[:, pl.ds(c0, qt)], p.astype(BF16),
                                        preferred_element_type=F32)
            return m_new, l, acc

        def kv_step(carry, b, slot):
            m, l, acc, mx_cur, blk_cur = carry
            c_next = pl.multiple_of(b * qt, qt)
            s_next = jnp.dot(kext_ref[pl.ds(c_next, qt), :], qext_t,
                             preferred_element_type=F32)
            bufs[1 - slot][...] = s_next
            mx_next = jnp.max(s_next, axis=0, keepdims=True)
            m, l, acc = absorb(m, l, acc, slot, mx_cur, blk_cur)
            return m, l, acc, mx_next, b

        def kv_pair(b0, carry):
            return kv_step(kv_step(carry, b0, 0), b0 + 1, 1)

        def kv_quad(t, carry):
            return kv_pair(4 * t + 2, kv_pair(4 * t, carry))

        init = (jnp.full((1, qt), NEG_BIG, F32), jnp.zeros((1, qt), F32),
                jnp.zeros((HEAD_DIM, qt), F32), mx_diag, a)
        carry = lax.fori_loop(0, a // 4, kv_quad, init)
        carry = lax.cond(a % 4 >= 2, lambda c: kv_pair(a // 4 * 4, c), lambda c: c, carry)

        def odd_tail(carry):
            m, l, acc, mx_last, blk_last = kv_step(carry, a - 1, 0)
            return absorb(m, l, acc, 1, mx_last, blk_last)

        def even_tail(carry):
            m, l, acc, mx_last, blk_last = carry
            return absorb(m, l, acc, 0, mx_last, blk_last)

        m, l, acc = lax.cond(a % 2 == 1, odd_tail, even_tail, carry)
        o_ref[pl.ds(row0, qt), :] = (acc * (1.0 / l)).T.astype(o_ref.dtype)

    def q_two(j, _):
        a0, a1 = 2 * j, 2 * j + 1
        qx0, mx0 = prologue(a0, sa_ref)
        qx1, mx1 = prologue(a1, sc_ref)
        attend(a0, qx0, mx0, (sa_ref, sb_ref))
        attend(a1, qx1, mx1, (sc_ref, sb_ref))
        return 0

    lax.fori_loop(0, nsb // 2, q_two, 0)


def _moba(qkv, bsz, seq):
    h = ATTN_HEADS
    kern = functools.partial(_moba_kernel, seq=seq)
    return pl.pallas_call(
        kern,
        out_shape=jax.ShapeDtypeStruct((bsz, seq, h * HEAD_DIM), BF16),
        grid=(bsz, h),
        in_specs=[pl.BlockSpec((None, seq, HEAD_DIM), lambda b, hh: (b, 0, hh)),
                  pl.BlockSpec((None, seq, HEAD_DIM), lambda b, hh: (b, 0, h + hh)),
                  pl.BlockSpec((None, seq, HEAD_DIM), lambda b, hh: (b, 0, 2 * h + hh))],
        out_specs=pl.BlockSpec((None, seq, HEAD_DIM), lambda b, hh: (b, 0, hh)),
        scratch_shapes=[pltpu.VMEM((seq, 2 * HEAD_DIM), BF16),
                        pltpu.VMEM((HEAD_DIM, seq), BF16),
                        pltpu.VMEM((2 * MOBA_BLOCK, 2 * MOBA_BLOCK), F32),
                        pltpu.VMEM((2 * MOBA_BLOCK, 2 * MOBA_BLOCK), F32),
                        pltpu.VMEM((2 * MOBA_BLOCK, 2 * MOBA_BLOCK), F32),
                        pltpu.VMEM((LANES, seq), BF16)],
        compiler_params=_cparams(("arbitrary", "arbitrary")),
        name="moba_attention",
    )(qkv, qkv, qkv)


def _lru_kernel(x_ref, y_ref, cw_ref, cb_ref, w_ref, bias_ref, lam_ref, o_ref,
                win_ref, a_ref, b_ref, h_ref, *, ts):
    t_idx = pl.program_id(2)

    @pl.when(t_idx == 0)
    def _():
        win_ref[0:SUBLANES, :] = jnp.zeros((SUBLANES, win_ref.shape[1]), F32)
        h_ref[...] = jnp.zeros_like(h_ref)

    win_ref[SUBLANES:, :] = x_ref[...]
    xc = cb_ref[...] + cw_ref[CONV_WIDTH - 1:CONV_WIDTH, :] * x_ref[...]
    for k in range(CONV_WIDTH - 1):
        shift = CONV_WIDTH - 1 - k
        xc = xc + cw_ref[k:k + 1, :] * win_ref[SUBLANES - shift:SUBLANES - shift + ts, :]
    win_ref[0:SUBLANES, :] = win_ref[ts:ts + SUBLANES, :]

    gates = jnp.dot(xc.astype(BF16), w_ref[...], preferred_element_type=F32) + bias_ref[...]
    cblk = xc.shape[1]
    rt = _sigmoid(gates[:, :cblk])
    it = _sigmoid(gates[:, cblk:])
    neg_lam = -lam_ref[...]
    softplus = jnp.maximum(neg_lam, 0.0) + jnp.log1p(jnp.exp(-jnp.abs(neg_lam)))
    log_a = (-LRU_C) * rt * softplus
    a = jnp.exp(log_a)
    a_ref[...] = a
    b_ref[...] = jnp.sqrt(-jnp.tanh(log_a) * (a * a + 1.0)) * (it * xc)

    row = lax.broadcasted_iota(jnp.int32, (SUBLANES, cblk), 0)

    def group(gi, h_in):
        r0 = pl.multiple_of(gi * SUBLANES, SUBLANES)
        a = a_ref[pl.ds(r0, SUBLANES), :]
        b = b_ref[pl.ds(r0, SUBLANES), :]
        for d in (1, 2, 4):
            keep = row >= d
            a_sh = jnp.where(keep, pltpu.roll(a, d, axis=0), 1.0)
            b_sh = jnp.where(keep, pltpu.roll(b, d, axis=0), 0.0)
            b = a * b_sh + b
            a = a * a_sh
        hs = b + a * h_in
        o_ref[pl.ds(r0, SUBLANES), :] = (hs * y_ref[pl.ds(r0, SUBLANES), :]).astype(o_ref.dtype)
        return jnp.broadcast_to(hs[SUBLANES - 1:SUBLANES, :], (SUBLANES, cblk))

    h_ref[...] = lax.fori_loop(0, ts // SUBLANES, group, h_ref[...], unroll=4)


def _lru(xr, yr, conv_w, conv_b, wa, ba, wx, bx, lam, bsz, seq, ts=2048):
    r = xr.shape[-1]
    ts = min(ts, seq)
    nblk = LRU_BLOCKS
    cblk = r // nblk
    w_cat = jnp.concatenate([wa, wx], axis=-1).astype(BF16)
    b_cat = jnp.concatenate([ba.reshape(nblk, 1, cblk), bx.reshape(nblk, 1, cblk)], axis=-1).astype(F32)
    kern = functools.partial(_lru_kernel, ts=ts)
    chan = lambda b, n, t: (0, n)
    return pl.pallas_call(
        kern,
        out_shape=jax.ShapeDtypeStruct((bsz, seq, r), BF16),
        grid=(bsz, nblk, seq // ts),
        in_specs=[pl.BlockSpec((None, ts, cblk), lambda b, n, t: (b, t, n)),
                  pl.BlockSpec((None, ts, cblk), lambda b, n, t: (b, t, n)),
                  pl.BlockSpec((CONV_WIDTH, cblk), chan),
                  pl.BlockSpec((1, cblk), chan),
                  pl.BlockSpec((None, cblk, 2 * cblk), lambda b, n, t: (n, 0, 0)),
                  pl.BlockSpec((None, 1, 2 * cblk), lambda b, n, t: (n, 0, 0)),
                  pl.BlockSpec((1, cblk), chan)],
        out_specs=pl.BlockSpec((None, ts, cblk), lambda b, n, t: (b, t, n)),
        scratch_shapes=[pltpu.VMEM((ts + SUBLANES, cblk), F32),
                        pltpu.VMEM((ts, cblk), F32),
                        pltpu.VMEM((ts, cblk), F32),
                        pltpu.VMEM((SUBLANES, cblk), F32)],
        compiler_params=_cparams(("parallel", "parallel", "arbitrary")),
        name="conv_rglru",
    )(xr, yr, conv_w.astype(F32), conv_b.reshape(1, r).astype(F32), w_cat, b_cat,
      lam.reshape(1, r).astype(F32))


def _merge_kernel(o_ref, hy_ref, wa_ref, wr_ref, ga_ref, gr_ref, out_ref):
    ya = jnp.dot(o_ref[...], wa_ref[...], preferred_element_type=F32)
    yr = jnp.dot(hy_ref[...], wr_ref[...], preferred_element_type=F32)
    out_ref[...] = (ga_ref[...] * ya + gr_ref[...] * yr).astype(out_ref.dtype)


def _merge(o, hy, wa, wr, gates, tm=1024, tn=512):
    t, k = o.shape
    kr = hy.shape[1]
    n = wa.shape[1]
    nj = n // tn
    return pl.pallas_call(
        _merge_kernel,
        out_shape=jax.ShapeDtypeStruct((t, n), BF16),
        grid=(t // tm, nj),
        in_specs=[pl.BlockSpec((tm, k), lambda i, j: (i, 0)),
                  pl.BlockSpec((tm, kr), lambda i, j: (i, 0)),
                  pl.BlockSpec((k, tn), lambda i, j: (0, j)),
                  pl.BlockSpec((kr, tn), lambda i, j: (0, j)),
                  pl.BlockSpec((tm, tn), lambda i, j: (i, j)),
                  pl.BlockSpec((tm, tn), lambda i, j: (i, nj + j))],
        out_specs=pl.BlockSpec((tm, tn), lambda i, j: (i, j)),
        compiler_params=_cparams(("parallel", "parallel")),
        name="branch_merge",
    )(o, hy, wa, wr, gates, gates)


def _pack_halves(x):
    bits = pltpu.bitcast(x.astype(F32), jnp.uint32)
    half = x.shape[1] // 2
    return (bits[:, :half] >> 16) | bits[:, half:]


def _unpack_halves(w):
    lo = pltpu.bitcast(w << 16, F32).astype(BF16)
    hi = pltpu.bitcast(w & jnp.uint32(0xFFFF0000), F32).astype(BF16)
    return jnp.concatenate([lo, hi], axis=1)


def _outproj_kernel(m_ref, wo_ref, x_ref, g_ref, rw_ref, rb_ref, x1_ref, hf_ref, lg_ref):
    x1 = x_ref[...] + jnp.dot(m_ref[...], wo_ref[...], preferred_element_type=F32)
    x1_ref[...] = x1
    ms = jnp.mean(x1 * x1, axis=-1, keepdims=True)
    hf = (x1 * lax.rsqrt(ms + RMS_EPS) * g_ref[...]).astype(BF16)
    hf_ref[...] = _pack_halves(hf)
    lg_ref[...] = jnp.dot(hf, rw_ref[...], preferred_element_type=F32) + rb_ref[...]


def _outproj(merged, wo, x, g, rw, rb, tm=512):
    t, d = x.shape
    ne = rw.shape[1]
    rw_p = jnp.zeros((d, LANES), BF16).at[:, :ne].set(rw.astype(BF16))
    rb_p = jnp.zeros((1, LANES), F32).at[0, :ne].set(rb.astype(F32))
    row = lambda i: (i, 0)
    fixed = lambda i: (0, 0)
    return pl.pallas_call(
        _outproj_kernel,
        out_shape=(jax.ShapeDtypeStruct((t, d), F32),
                   jax.ShapeDtypeStruct((t, d // 2), jnp.uint32),
                   jax.ShapeDtypeStruct((t, LANES), F32)),
        grid=(t // tm,),
        in_specs=[pl.BlockSpec((tm, d), row),
                  pl.BlockSpec((d, d), fixed),
                  pl.BlockSpec((tm, d), row),
                  pl.BlockSpec((1, d), fixed),
                  pl.BlockSpec((d, LANES), fixed),
                  pl.BlockSpec((1, LANES), fixed)],
        out_specs=(pl.BlockSpec((tm, d), row),
                   pl.BlockSpec((tm, d // 2), row),
                   pl.BlockSpec((tm, LANES), row)),
        compiler_params=_cparams(("parallel",)),
        name="outproj_norm_router",
    )(merged, wo, x, g.reshape(1, d).astype(F32), rw_p, rb_p)


def _expert_changed(be_ref, r):
    return jnp.logical_or(r == 0, be_ref[r] != be_ref[jnp.maximum(r - 1, 0)])


def _expert_up_kernel(be_ref, nu_ref, xs_ref, wg_ref, bg_ref, wu_ref, bu_ref, act_ref,
                      wgb_ref, wub_ref):
    r = pl.program_id(1)
    used = r < nu_ref[0]

    @pl.when(jnp.logical_and(used, _expert_changed(be_ref, r)))
    def _():
        wgb_ref[...] = wg_ref[...].astype(BF16)
        wub_ref[...] = wu_ref[...].astype(BF16)

    @pl.when(used)
    def _():
        xs = _unpack_halves(xs_ref[...])
        glu = jnp.minimum(jnp.dot(xs, wgb_ref[...], preferred_element_type=F32) + bg_ref[...],
                          SWIGLU_LIMIT)
        lin = jnp.clip(jnp.dot(xs, wub_ref[...], preferred_element_type=F32) + bu_ref[...],
                       -SWIGLU_LIMIT, SWIGLU_LIMIT)
        act_ref[...] = (glu * _sigmoid(SWIGLU_ALPHA * glu) * (lin + 1.0)).astype(act_ref.dtype)

    @pl.when(jnp.logical_not(used))
    def _():
        act_ref[...] = jnp.zeros_like(act_ref)


def _expert_down_kernel(be_ref, nu_ref, act_ref, wd_ref, bd_ref, y_ref, wdb_ref):
    r = pl.program_id(1)
    used = r < nu_ref[0]

    @pl.when(jnp.logical_and(used, _expert_changed(be_ref, r)))
    def _():
        wdb_ref[...] = wd_ref[...].astype(BF16)

    @pl.when(used)
    def _():
        y_ref[...] = jnp.dot(act_ref[...], wdb_ref[...], preferred_element_type=F32) + bd_ref[...]

    @pl.when(jnp.logical_not(used))
    def _():
        y_ref[...] = jnp.zeros_like(y_ref)


def _experts(xs, block_expert, n_used, wg, bg, wu, bu, wd, bd, tf=1024, tn=2048):
    p = xs.shape[0]
    ne, d, dff = wg.shape
    tf, tn = min(tf, dff), min(tn, d)
    nblk = p // MOE_BLOCK

    def rr(r, nu):
        return jnp.minimum(r, nu[0] - 1)

    wspec = lambda k, t: pl.BlockSpec((None, k, t), lambda c, r, be, nu: (be[rr(r, nu)], 0, c))
    bspec = lambda t: pl.BlockSpec((None, 1, t), lambda c, r, be, nu: (be[rr(r, nu)], 0, c))
    rows = lambda k: pl.BlockSpec((MOE_BLOCK, k), lambda c, r, be, nu: (rr(r, nu), 0))

    act = pl.pallas_call(
        _expert_up_kernel,
        out_shape=jax.ShapeDtypeStruct((p, dff), BF16),
        grid_spec=pltpu.PrefetchScalarGridSpec(
            num_scalar_prefetch=2,
            grid=(dff // tf, nblk),
            in_specs=[rows(d // 2), wspec(d, tf), bspec(tf), wspec(d, tf), bspec(tf)],
            out_specs=pl.BlockSpec((MOE_BLOCK, tf), lambda c, r, be, nu: (r, c)),
            scratch_shapes=[pltpu.VMEM((d, tf), BF16), pltpu.VMEM((d, tf), BF16)]),
        compiler_params=_cparams(("arbitrary", "arbitrary")),
        name="moe_up",
    )(block_expert, n_used, xs, wg, bg.reshape(ne, 1, dff).astype(F32), wu,
      bu.reshape(ne, 1, dff).astype(F32))

    return pl.pallas_call(
        _expert_down_kernel,
        out_shape=jax.ShapeDtypeStruct((p, d), F32),
        grid_spec=pltpu.PrefetchScalarGridSpec(
            num_scalar_prefetch=2,
            grid=(d // tn, nblk),
            in_specs=[rows(dff), wspec(dff, tn), bspec(tn)],
            out_specs=pl.BlockSpec((MOE_BLOCK, tn), lambda c, r, be, nu: (r, c)),
            scratch_shapes=[pltpu.VMEM((dff, tn), BF16)]),
        compiler_params=_cparams(("arbitrary", "arbitrary")),
        name="moe_down",
    )(block_expert, n_used, act, wd, bd.reshape(ne, 1, d).astype(F32))


def _combine_kernel(dest_ref, x1_ref, gw_ref, g_ref, y_hbm, o_ref, *scratch, tt, ct, norm):
    nch = tt // ct
    bufs, sem_ref = scratch[:nch], scratch[nch]
    i = pl.program_id(0)
    last = pl.num_programs(0) - 1
    lead = 2
    grp = 2 * SUBLANES

    def issue_rows(tok0, n, slot, asg0):
        for j in range(n):
            for k in range(MOE_TOPK):
                row = dest_ref[asg0 + j * MOE_TOPK + k]
                pltpu.make_async_copy(y_hbm.at[pl.ds(row, 1), :],
                                      bufs[slot].at[k, pl.ds(tok0 + j, 1), :],
                                      sem_ref.at[slot]).start()

    def issue_chunk(step, c):
        asg = (step * tt + c * ct) * MOE_TOPK

        def body(g, _):
            tok0 = pl.multiple_of(g * grp, grp)
            issue_rows(tok0, grp, c, asg + tok0 * MOE_TOPK)
            return 0
        lax.fori_loop(0, ct // grp, body, 0)

    def reduce_group(c, tok0):
        rows = pl.ds(c * ct + tok0, grp)
        acc = x1_ref[rows, :]
        gw = gw_ref[rows, :]
        for k in range(MOE_TOPK):
            acc = acc + gw[:, k:k + 1] * bufs[c][k, pl.ds(tok0, grp), :]
        if norm:
            ms = jnp.mean(acc * acc, axis=-1, keepdims=True)
            acc = acc * lax.rsqrt(ms + RMS_EPS) * g_ref[...]
        o_ref[rows, :] = acc

    @pl.when(i == 0)
    def _():
        for c in range(lead):
            issue_chunk(0, c)

    for c in range(nch):
        for k in range(MOE_TOPK):
            pltpu.make_async_copy(y_hbm.at[pl.ds(0, ct), :], bufs[c].at[k], sem_ref.at[c]).wait()
        c_next = (c + lead) % nch
        step_next = i + (c + lead) // nch

        def reduce_and_issue(g, _, c=c, c_next=c_next, step_next=step_next):
            tok0 = pl.multiple_of(g * grp, grp)
            issue_rows(tok0, grp, c_next, ((step_next * tt + c_next * ct) + tok0) * MOE_TOPK)
            reduce_group(c, tok0)
            return 0

        def reduce_only(g, _, c=c):
            reduce_group(c, pl.multiple_of(g * grp, grp))
            return 0

        if c + lead < nch:
            lax.fori_loop(0, ct // grp, reduce_and_issue, 0)
        else:
            @pl.when(i < last)
            def _(f=reduce_and_issue):
                lax.fori_loop(0, ct // grp, f, 0)

            @pl.when(i == last)
            def _(f=reduce_only):
                lax.fori_loop(0, ct // grp, f, 0)


def _combine(x1, y, dest, gate_w, g, norm, tt=256, ct=64):
    t, d = x1.shape
    nch = tt // ct
    kern = functools.partial(_combine_kernel, tt=tt, ct=ct, norm=norm)
    return pl.pallas_call(
        kern,
        out_shape=jax.ShapeDtypeStruct((t, d), F32),
        grid_spec=pltpu.PrefetchScalarGridSpec(
            num_scalar_prefetch=1,
            grid=(t // tt,),
            in_specs=[pl.BlockSpec((tt, d), lambda i, dref: (i, 0)),
                      pl.BlockSpec((tt, MOE_TOPK), lambda i, dref: (i, 0)),
                      pl.BlockSpec((1, d), lambda i, dref: (0, 0)),
                      pl.BlockSpec(memory_space=pl.ANY)],
            out_specs=pl.BlockSpec((tt, d), lambda i, dref: (i, 0)),
            scratch_shapes=[pltpu.VMEM((MOE_TOPK, ct, d), F32) for _ in range(nch)]
            + [pltpu.SemaphoreType.DMA((nch,))]),
        compiler_params=_cparams(("arbitrary",)),
        name="moe_combine",
    )(dest.reshape(t * MOE_TOPK), x1, gate_w, g.reshape(1, d).astype(F32), y)


def _route_kernel(lg_ref, out_ref, cnt_ref, base_ref, *, n_experts, tm):
    @pl.when(pl.program_id(0) == 0)
    def _():
        base_ref[...] = jnp.zeros_like(base_ref)

    lane = lax.broadcasted_iota(jnp.int32, (tm, LANES), 1)
    lg = jnp.where(lane < n_experts, lg_ref[...], -jnp.inf)
    vals, ids = [], []
    onehot = jnp.zeros((tm, LANES), jnp.bool_)
    for _k in range(MOE_TOPK):
        mx = jnp.max(lg, axis=1, keepdims=True)
        first = jnp.min(jnp.where(lg == mx, lane, LANES), axis=1, keepdims=True)
        pick = lane == first
        onehot = jnp.logical_or(onehot, pick)
        lg = jnp.where(pick, -jnp.inf, lg)
        vals.append(mx)
        ids.append(first)
    exps = [jnp.exp(v - vals[0]) for v in vals]
    denom = exps[0] + exps[1] + exps[2] + exps[3]

    oh = jnp.where(onehot, 1.0, 0.0)
    r_i = lax.broadcasted_iota(jnp.int32, (tm, tm), 0)
    c_i = lax.broadcasted_iota(jnp.int32, (tm, tm), 1)
    earlier = jnp.where(r_i > c_i, 1.0, 0.0).astype(BF16)
    before = base_ref[...] + jnp.dot(earlier, oh.astype(BF16), preferred_element_type=F32)
    base_ref[...] = base_ref[...] + jnp.sum(oh, axis=0, keepdims=True)
    cnt_ref[...] = base_ref[...]

    out = jnp.zeros((tm, LANES), F32)
    for k in range(MOE_TOPK):
        rank = jnp.sum(jnp.where(lane == ids[k], before, 0.0), axis=1, keepdims=True)
        out = jnp.where(lane == k, exps[k] / denom, out)
        out = jnp.where(lane == MOE_TOPK + k, ids[k].astype(F32), out)
        out = jnp.where(lane == 2 * MOE_TOPK + k, rank, out)
    out_ref[...] = out


def _route(logits, n_experts, tm=512):
    t = logits.shape[0]
    kern = functools.partial(_route_kernel, n_experts=n_experts, tm=tm)
    packed, counts = pl.pallas_call(
        kern,
        out_shape=(jax.ShapeDtypeStruct((t, LANES), F32), jax.ShapeDtypeStruct((1, LANES), F32)),
        grid=(t // tm,),
        in_specs=[pl.BlockSpec((tm, LANES), lambda i: (i, 0))],
        out_specs=(pl.BlockSpec((tm, LANES), lambda i: (i, 0)),
                   pl.BlockSpec((1, LANES), lambda i: (0, 0))),
        scratch_shapes=[pltpu.VMEM((1, LANES), F32)],
        compiler_params=_cparams(("arbitrary",)),
        name="moe_route",
    )(logits)
    blk = MOE_BLOCK
    n_asg = t * MOE_TOPK
    p = (-(-n_asg // blk)) * blk + n_experts * blk
    cnt = counts[0, :n_experts].astype(jnp.int32)
    padded = (cnt + blk - 1) // blk * blk
    pad_end = jnp.cumsum(padded)
    pad_start = pad_end - padded
    row_starts = jnp.arange(p // blk, dtype=jnp.int32) * blk
    block_expert = jnp.minimum(jnp.sum(pad_end[None, :] <= row_starts[:, None], axis=1),
                               n_experts - 1).astype(jnp.int32)
    n_used = (pad_end[-1] // blk).astype(jnp.int32).reshape(1)
    gate_w = packed[:, :MOE_TOPK]
    ids = packed[:, MOE_TOPK:2 * MOE_TOPK].astype(jnp.int32)
    rank = packed[:, 2 * MOE_TOPK:3 * MOE_TOPK].astype(jnp.int32)
    sel = ids[:, :, None] == jnp.arange(n_experts, dtype=jnp.int32)[None, None, :]
    dest = rank + jnp.sum(jnp.where(sel, pad_start[None, None, :], 0), axis=-1)
    fill_start = jnp.minimum((pad_start + cnt) // SUBLANES * SUBLANES, p - blk).astype(jnp.int32)
    return gate_w, dest, block_expert, n_used, fill_start, p


def _dispatch_kernel(dest_ref, fill_ref, nu_ref, hf_ref, xs_hbm, zero_ref, sem_ref,
                     *, tt, n_experts, nblk):
    i = pl.program_id(0)

    def zero_fill(start):
        return pltpu.make_async_copy(zero_ref, xs_hbm.at[pl.ds(start, MOE_BLOCK), :],
                                     sem_ref.at[0])

    @pl.when(i == 0)
    def _():
        zero_ref[...] = jnp.zeros_like(zero_ref)
        for e in range(n_experts):
            fill = zero_fill(pl.multiple_of(fill_ref[e], SUBLANES))
            fill.start()
            fill.wait()

        def fill_block(r, _):
            fill = zero_fill(pl.multiple_of(r * MOE_BLOCK, MOE_BLOCK))
            fill.start()
            fill.wait()
            return 0

        lax.fori_loop(nu_ref[0], nblk, fill_block, 0)

    base = i * tt

    def body(tok, _):
        for k in range(MOE_TOPK):
            row = dest_ref[(base + tok) * MOE_TOPK + k]
            pltpu.make_async_copy(hf_ref.at[pl.ds(tok, 1), :],
                                  xs_hbm.at[pl.ds(row, 1), :], sem_ref.at[1]).start()
        return 0

    lax.fori_loop(0, tt, body, 0, unroll=2)
    for k in range(MOE_TOPK):
        pltpu.make_async_copy(hf_ref, xs_hbm.at[pl.ds(0, tt), :], sem_ref.at[1]).wait()


def _dispatch(hfp, dest, fill_start, n_used, p, n_experts, tt=2048):
    t, w = hfp.shape
    tt = min(tt, t)
    kern = functools.partial(_dispatch_kernel, tt=tt, n_experts=n_experts, nblk=p // MOE_BLOCK)
    return pl.pallas_call(
        kern,
        out_shape=jax.ShapeDtypeStruct((p, w), hfp.dtype),
        grid_spec=pltpu.PrefetchScalarGridSpec(
            num_scalar_prefetch=3,
            grid=(t // tt,),
            in_specs=[pl.BlockSpec((tt, w), lambda i, dref, fref, nref: (i, 0))],
            out_specs=pl.BlockSpec(memory_space=pl.ANY),
            scratch_shapes=[pltpu.VMEM((MOE_BLOCK, w), hfp.dtype),
                            pltpu.SemaphoreType.DMA((2,))]),
        compiler_params=_cparams(("arbitrary",)),
        name="moe_dispatch",
    )(dest.reshape(t * MOE_TOPK), fill_start, n_used, hfp)


def _layer(x2, bsz, seq, norm_mix_g, w_in, conv_w, conv_b, lru_wa, lru_ba, lru_wx, lru_bx,
           lru_lambda, w_attn_branch, w_rnn_branch, w_out, norm_ffn_g, router_w, router_b,
           w_gate, b_gate, w_up, b_up, w_down, b_down):
    t, d = x2.shape
    aw = ATTN_HEADS * HEAD_DIM
    r = lru_lambda.shape[0]
    ne = router_w.shape[1]

    hm = _rmsnorm(x2, norm_mix_g, BF16)
    w_in_b = w_in.astype(BF16)
    qkv = _proj(hm, w_in_b, 0, 3 * aw, BF16, "scale_head",
                n_scaled=aw, scale=HEAD_DIM ** -0.5 * LOG2E)
    xr = _proj(hm, w_in_b, 3 * aw, r, F32)
    yr = _proj(hm, w_in_b, 3 * aw + r, r, F32, "gelu")
    gates = _proj(hm, w_in_b, 3 * aw + 2 * r, 2 * d, F32, "sigmoid")

    o = _moba(qkv.reshape(bsz, seq, 3 * aw), bsz, seq).reshape(t, aw)
    hy = _lru(xr.reshape(bsz, seq, r), yr.reshape(bsz, seq, r), conv_w, conv_b,
              lru_wa, lru_ba, lru_wx, lru_bx, lru_lambda, bsz, seq).reshape(t, r)

    merged = _merge(o, hy, w_attn_branch.astype(BF16), w_rnn_branch.astype(BF16), gates)
    x1, hf, logits = _outproj(merged, w_out.astype(BF16), x2, norm_ffn_g, router_w, router_b)

    gate_w, dest, block_expert, n_used, fill_start, p = _route(logits, ne)
    xs = _dispatch(hf, dest, fill_start, n_used, p, ne)
    y = _experts(xs, block_expert, n_used, w_gate, b_gate, w_up, b_up, w_down, b_down)
    return x1, y, dest, gate_w


def kernel(x, norm_mix_g, w_in, conv_w, conv_b, lru_wa, lru_ba, lru_wx, lru_bx, lru_lambda, w_attn_branch, w_rnn_branch, w_out, norm_ffn_g, router_w, router_b, w_gate, b_gate, w_up, b_up, w_down, b_down, norm_final_g):
    bsz, seq, d = x.shape
    depth = w_in.shape[0]
    assert seq % (4 * MOBA_BLOCK) == 0 and seq // MOBA_BLOCK <= LANES and d % LANES == 0
    x2 = x.reshape(bsz * seq, d)
    for l in range(depth):
        x1, y, dest, gate_w = _layer(
            x2, bsz, seq, norm_mix_g[l], w_in[l], conv_w[l], conv_b[l], lru_wa[l], lru_ba[l],
            lru_wx[l], lru_bx[l], lru_lambda[l], w_attn_branch[l], w_rnn_branch[l], w_out[l],
            norm_ffn_g[l], router_w[l], router_b[l], w_gate[l], b_gate[l], w_up[l], b_up[l],
            w_down[l], b_down[l])
        last = l + 1 == depth
        x2 = _combine(x1, y, dest, gate_w, norm_final_g, norm=last)
    return x2.reshape(bsz, seq, d)
```

```python
import functools
import math

import jax
import jax.numpy as jnp
from jax import lax
from jax.experimental import pallas as pl
from jax.experimental.pallas import tpu as pltpu

ATTN_HEADS = 16
HEAD_DIM = 128
MOBA_BLOCK = 256
MOBA_TOPK = 3
LRU_BLOCKS = 8
CONV_WIDTH = 4
LRU_C = 8.0
MOE_TOPK = 4
SWIGLU_ALPHA = 1.702
SWIGLU_LIMIT = 7.0
MOE_BLOCK = 512
RMS_EPS = 1e-6

LANES = 128
SUBLANES = 8
VMEM_LIMIT = 56 * 1024 * 1024
LOG2E = 1.4426950408889634
NEG_BIG = -1e30

BF16 = jnp.bfloat16
F32 = jnp.float32


def _cparams(sem):
    return pltpu.CompilerParams(dimension_semantics=sem, vmem_limit_bytes=VMEM_LIMIT)


def _sigmoid(x):
    return 1.0 / (1.0 + jnp.exp(-x))


def _gelu_tanh(x):
    c = 0.7978845608028654
    return 0.5 * x * (1.0 + jnp.tanh(c * (x + 0.044715 * (x * x * x))))


def _rmsnorm_kernel(x_ref, g_ref, o_ref):
    x = x_ref[...]
    ms = jnp.mean(x * x, axis=-1, keepdims=True)
    o_ref[...] = (x * lax.rsqrt(ms + RMS_EPS) * g_ref[...]).astype(o_ref.dtype)


def _rmsnorm(x, g, out_dtype, tm=1024):
    t, d = x.shape
    return pl.pallas_call(
        _rmsnorm_kernel,
        out_shape=jax.ShapeDtypeStruct((t, d), out_dtype),
        grid=(t // tm,),
        in_specs=[pl.BlockSpec((tm, d), lambda i: (i, 0)),
                  pl.BlockSpec((1, d), lambda i: (0, 0))],
        out_specs=pl.BlockSpec((tm, d), lambda i: (i, 0)),
        compiler_params=_cparams(("parallel",)),
        name="rmsnorm",
    )(x, g.reshape(1, d).astype(F32))


def _proj_kernel(a_ref, w_ref, o_ref, *, epilogue, n_scaled, scale):
    acc = jnp.dot(a_ref[...], w_ref[...], preferred_element_type=F32)
    if epilogue == "scale_head":
        n_scaled_tiles = n_scaled // o_ref.shape[1]
        s = jnp.where(pl.program_id(1) < n_scaled_tiles, scale, 1.0).astype(F32)
        acc = acc * s
    elif epilogue == "gelu":
        acc = _gelu_tanh(acc)
    elif epilogue == "sigmoid":
        acc = _sigmoid(acc)
    o_ref[...] = acc.astype(o_ref.dtype)


def _proj(a, w, col0, n, out_dtype, epilogue="none", n_scaled=0, scale=1.0,
          tm=1024, tn=2048):
    t, k = a.shape
    tn = math.gcd(tn, col0, n, n_scaled)
    c0 = col0 // tn
    kern = functools.partial(_proj_kernel, epilogue=epilogue, n_scaled=n_scaled, scale=scale)
    return pl.pallas_call(
        kern,
        out_shape=jax.ShapeDtypeStruct((t, n), out_dtype),
        grid=(t // tm, n // tn),
        in_specs=[pl.BlockSpec((tm, k), lambda i, j: (i, 0)),
                  pl.BlockSpec((k, tn), lambda i, j: (0, c0 + j))],
        out_specs=pl.BlockSpec((tm, tn), lambda i, j: (i, j)),
        compiler_params=_cparams(("parallel", "parallel")),
        name="proj_" + epilogue,
    )(a, w)


def _moba_kernel(q_ref, k_ref, v_ref, o_ref, kext_ref, vt_ref, sa_ref, sb_ref, sc_ref, avg_ref,
                 *, seq):
    blk = MOBA_BLOCK
    qt = 2 * blk
    nsb = seq // qt

    @pl.when(jnp.logical_and(pl.program_id(0) == 0, pl.program_id(1) == 0))
    def _():
        r_blk = lax.broadcasted_iota(jnp.int32, (seq, LANES), 0) // blk
        c_id = lax.broadcasted_iota(jnp.int32, (seq, LANES), 1)
        kext_ref[:, HEAD_DIM:] = jnp.where(r_blk == c_id, 1.0, 0.0).astype(BF16)
        a_row = lax.broadcasted_iota(jnp.int32, (LANES, seq), 0)
        a_col = lax.broadcasted_iota(jnp.int32, (LANES, seq), 1) // blk
        avg_ref[...] = jnp.where(a_row == a_col, 1.0 / blk, 0.0).astype(BF16)

    kext_ref[:, :HEAD_DIM] = k_ref[...]
    vt_ref[...] = v_ref[...].T
    kmean = jnp.dot(avg_ref[...], k_ref[...], preferred_element_type=F32).astype(BF16)

    nrow = -(-(seq // blk) // SUBLANES) * SUBLANES
    brow = lax.broadcasted_iota(jnp.int32, (nrow, qt), 0)
    second = lax.broadcasted_iota(jnp.int32, (nrow, qt), 1) >= blk
    key_i = lax.broadcasted_iota(jnp.int32, (qt, qt), 0)
    qry_i = lax.broadcasted_iota(jnp.int32, (qt, qt), 1)
    same_blk = (key_i >= blk) == (qry_i >= blk)
    causal_self = jnp.logical_and(same_blk, key_i <= qry_i)
    cross = jnp.logical_and(key_i < blk, qry_i >= blk)

    def prologue(a, diag_ref):
        row0 = pl.multiple_of(a * qt, qt)
        q = q_ref[pl.ds(row0, qt), :]
        q_t = q.T
        gate = jnp.dot(kmean, q_t, preferred_element_type=F32)[:nrow, :]
        qblk = 2 * a + second.astype(jnp.int32)
        past = brow < qblk
        g = jnp.where(past, gate, -jnp.inf)
        sel = jnp.zeros((nrow, qt), jnp.bool_)
        for _k in range(MOBA_TOPK):
            mx = jnp.max(g, axis=0, keepdims=True)
            first = jnp.min(jnp.where(g == mx, brow, nrow), axis=0, keepdims=True)
            pick = brow == first
            sel = jnp.logical_or(sel, pick)
            g = jnp.where(pick, -jnp.inf, g)
        chosen = jnp.logical_and(sel, past)
        bias_t = jnp.where(chosen, 0.0, NEG_BIG)
        bias_t = jnp.concatenate([bias_t, jnp.zeros((LANES - nrow, qt), F32)], axis=0)
        qext_t = jnp.concatenate([q_t, bias_t.astype(BF16)], axis=0)

        prev_sel = jnp.max(jnp.where(jnp.logical_and(chosen, brow == 2 * a), 1.0, 0.0),
                           axis=0, keepdims=True) > 0.5
        allowed = jnp.logical_or(causal_self, jnp.logical_and(cross, prev_sel))
        s_diag = jnp.where(allowed, jnp.dot(k_ref[pl.ds(row0, qt), :], q_t,
                                            preferred_element_type=F32), NEG_BIG)
        diag_ref[...] = s_diag
        return qext_t, jnp.max(s_diag, axis=0, keepdims=True)

    def attend(a, qext_t, mx_diag, bufs):
        row0 = pl.multiple_of(a * qt, qt)

        def absorb(m, l, acc, slot, mx, blk_idx):
            c0 = pl.multiple_of(blk_idx * qt, qt)
            m_new = jnp.maximum(m, mx)
            alpha = jnp.exp2(m - m_new)
            p = jnp.exp2(bufs[slot][...] - m_new)
            l = alpha * l + jnp.sum(p, axis=0, keepdims=True)
            acc = alpha * acc + jnp.dot(vt_ref[:, pl.ds(c0, qt)], p.astype(BF16),
                                        preferred_element_type=F32)
            return m_new, l, acc

        def kv_step(carry, b, slot):
            m, l, acc, mx_cur, blk_cur = carry
            c_next = pl.multiple_of(b * qt, qt)
            s_next = jnp.dot(kext_ref[pl.ds(c_next, qt), :], qext_t,
                             preferred_element_type=F32)
            bufs[1 - slot][...] = s_next
            mx_next = jnp.max(s_next, axis=0, keepdims=True)
            m, l, acc = absorb(m, l, acc, slot, mx_cur, blk_cur)
            return m, l, acc, mx_next, b

        def kv_pair(b0, carry):
            return kv_step(kv_step(carry, b0, 0), b0 + 1, 1)

        def kv_quad(t, carry):
            return kv_pair(4 * t + 2, kv_pair(4 * t, carry))

        init = (jnp.full((1, qt), NEG_BIG, F32), jnp.zeros((1, qt), F32),
                jnp.zeros((HEAD_DIM, qt), F32), mx_diag, a)
        carry = lax.fori_loop(0, a // 4, kv_quad, init)
        carry = lax.cond(a % 4 >= 2, lambda c: kv_pair(a // 4 * 4, c), lambda c: c, carry)

        def odd_tail(carry):
            m, l, acc, mx_last, blk_last = kv_step(carry, a - 1, 0)
            return absorb(m, l, acc, 1, mx_last, blk_last)

        def even_tail(carry):
            m, l, acc, mx_last, blk_last = carry
            return absorb(m, l, acc, 0, mx_last, blk_last)

        m, l, acc = lax.cond(a % 2 == 1, odd_tail, even_tail, carry)
        o_ref[pl.ds(row0, qt), :] = (acc * (1.0 / l)).T.astype(o_ref.dtype)

    def q_two(j, _):
        a0, a1 = 2 * j, 2 * j + 1
        qx0, mx0 = prologue(a0, sa_ref)
        qx1, mx1 = prologue(a1, sc_ref)
        attend(a0, qx0, mx0, (sa_ref, sb_ref))
        attend(a1, qx1, mx1, (sc_ref, sb_ref))
        return 0

    lax.fori_loop(0, nsb // 2, q_two, 0)


def _moba(qkv, bsz, seq):
    h = ATTN_HEADS
    kern = functools.partial(_moba_kernel, seq=seq)
    return pl.pallas_call(
        kern,
        out_shape=jax.ShapeDtypeStruct((bsz, seq, h * HEAD_DIM), BF16),
        grid=(bsz, h),
        in_specs=[pl.BlockSpec((None, seq, HEAD_DIM), lambda b, hh: (b, 0, hh)),
                  pl.BlockSpec((None, seq, HEAD_DIM), lambda b, hh: (b, 0, h + hh)),
                  pl.BlockSpec((None, seq, HEAD_DIM), lambda b, hh: (b, 0, 2 * h + hh))],
        out_specs=pl.BlockSpec((None, seq, HEAD_DIM), lambda b, hh: (b, 0, hh)),
        scratch_shapes=[pltpu.VMEM((seq, 2 * HEAD_DIM), BF16),
                        pltpu.VMEM((HEAD_DIM, seq), BF16),
                        pltpu.VMEM((2 * MOBA_BLOCK, 2 * MOBA_BLOCK), F32),
                        pltpu.VMEM((2 * MOBA_BLOCK, 2 * MOBA_BLOCK), F32),
                        pltpu.VMEM((2 * MOBA_BLOCK, 2 * MOBA_BLOCK), F32),
                        pltpu.VMEM((LANES, seq), BF16)],
        compiler_params=_cparams(("arbitrary", "arbitrary")),
        name="moba_attention",
    )(qkv, qkv, qkv)


def _lru_kernel(x_ref, y_ref, cw_ref, cb_ref, w_ref, bias_ref, lam_ref, o_ref,
                win_ref, a_ref, b_ref, h_ref, *, ts):
    t_idx = pl.program_id(2)

    @pl.when(t_idx == 0)
    def _():
        win_ref[0:SUBLANES, :] = jnp.zeros((SUBLANES, win_ref.shape[1]), F32)
        h_ref[...] = jnp.zeros_like(h_ref)

    win_ref[SUBLANES:, :] = x_ref[...]
    xc = cb_ref[...] + cw_ref[CONV_WIDTH - 1:CONV_WIDTH, :] * x_ref[...]
    for k in range(CONV_WIDTH - 1):
        shift = CONV_WIDTH - 1 - k
        xc = xc + cw_ref[k:k + 1, :] * win_ref[SUBLANES - shift:SUBLANES - shift + ts, :]
    win_ref[0:SUBLANES, :] = win_ref[ts:ts + SUBLANES, :]

    gates = jnp.dot(xc.astype(BF16), w_ref[...], preferred_element_type=F32) + bias_ref[...]
    cblk = xc.shape[1]
    rt = _sigmoid(gates[:, :cblk])
    it = _sigmoid(gates[:, cblk:])
    neg_lam = -lam_ref[...]
    softplus = jnp.maximum(neg_lam, 0.0) + jnp.log1p(jnp.exp(-jnp.abs(neg_lam)))
    log_a = (-LRU_C) * rt * softplus
    a = jnp.exp(log_a)
    a_ref[...] = a
    b_ref[...] = jnp.sqrt(-jnp.tanh(log_a) * (a * a + 1.0)) * (it * xc)

    row = lax.broadcasted_iota(jnp.int32, (SUBLANES, cblk), 0)

    def group(gi, h_in):
        r0 = pl.multiple_of(gi * SUBLANES, SUBLANES)
        a = a_ref[pl.ds(r0, SUBLANES), :]
        b = b_ref[pl.ds(r0, SUBLANES), :]
        for d in (1, 2, 4):
            keep = row >= d
            a_sh = jnp.where(keep, pltpu.roll(a, d, axis=0), 1.0)
            b_sh = jnp.where(keep, pltpu.roll(b, d, axis=0), 0.0)
            b = a * b_sh + b
            a = a * a_sh
        hs = b + a * h_in
        o_ref[pl.ds(r0, SUBLANES), :] = (hs * y_ref[pl.ds(r0, SUBLANES), :]).astype(o_ref.dtype)
        return jnp.broadcast_to(hs[SUBLANES - 1:SUBLANES, :], (SUBLANES, cblk))

    h_ref[...] = lax.fori_loop(0, ts // SUBLANES, group, h_ref[...], unroll=4)


def _lru(xr, yr, conv_w, conv_b, wa, ba, wx, bx, lam, bsz, seq, ts=2048):
    r = xr.shape[-1]
    ts = min(ts, seq)
    nblk = LRU_BLOCKS
    cblk = r // nblk
    w_cat = jnp.concatenate([wa, wx], axis=-1).astype(BF16)
    b_cat = jnp.concatenate([ba.reshape(nblk, 1, cblk), bx.reshape(nblk, 1, cblk)], axis=-1).astype(F32)
    kern = functools.partial(_lru_kernel, ts=ts)
    chan = lambda b, n, t: (0, n)
    return pl.pallas_call(
        kern,
        out_shape=jax.ShapeDtypeStruct((bsz, seq, r), BF16),
        grid=(bsz, nblk, seq // ts),
        in_specs=[pl.BlockSpec((None, ts, cblk), lambda b, n, t: (b, t, n)),
                  pl.BlockSpec((None, ts, cblk), lambda b, n, t: (b, t, n)),
                  pl.BlockSpec((CONV_WIDTH, cblk), chan),
                  pl.BlockSpec((1, cblk), chan),
                  pl.BlockSpec((None, cblk, 2 * cblk), lambda b, n, t: (n, 0, 0)),
                  pl.BlockSpec((None, 1, 2 * cblk), lambda b, n, t: (n, 0, 0)),
                  pl.BlockSpec((1, cblk), chan)],
        out_specs=pl.BlockSpec((None, ts, cblk), lambda b, n, t: (b, t, n)),
        scratch_shapes=[pltpu.VMEM((ts + SUBLANES, cblk), F32),
                        pltpu.VMEM((ts, cblk), F32),
                        pltpu.VMEM((ts, cblk), F32),
                        pltpu.VMEM((SUBLANES, cblk), F32)],
        compiler_params=_cparams(("parallel", "parallel", "arbitrary")),
        name="conv_rglru",
    )(xr, yr, conv_w.astype(F32), conv_b.reshape(1, r).astype(F32), w_cat, b_cat,
      lam.reshape(1, r).astype(F32))


def _merge_kernel(o_ref, hy_ref, wa_ref, wr_ref, ga_ref, gr_ref, out_ref):
    ya = jnp.dot(o_ref[...], wa_ref[...], preferred_element_type=F32)
    yr = jnp.dot(hy_ref[...], wr_ref[...], preferred_element_type=F32)
    out_ref[...] = (ga_ref[...] * ya + gr_ref[...] * yr).astype(out_ref.dtype)


def _merge(o, hy, wa, wr, gates, tm=1024, tn=512):
    t, k = o.shape
    kr = hy.shape[1]
    n = wa.shape[1]
    nj = n // tn
    return pl.pallas_call(
        _merge_kernel,
        out_shape=jax.ShapeDtypeStruct((t, n), BF16),
        grid=(t // tm, nj),
        in_specs=[pl.BlockSpec((tm, k), lambda i, j: (i, 0)),
                  pl.BlockSpec((tm, kr), lambda i, j: (i, 0)),
                  pl.BlockSpec((k, tn), lambda i, j: (0, j)),
                  pl.BlockSpec((kr, tn), lambda i, j: (0, j)),
                  pl.BlockSpec((tm, tn), lambda i, j: (i, j)),
                  pl.BlockSpec((tm, tn), lambda i, j: (i, nj + j))],
        out_specs=pl.BlockSpec((tm, tn), lambda i, j: (i, j)),
        compiler_params=_cparams(("parallel", "parallel")),
        name="branch_merge",
    )(o, hy, wa, wr, gates, gates)


def _pack_halves(x):
    bits = pltpu.bitcast(x.astype(F32), jnp.uint32)
    half = x.shape[1] // 2
    return (bits[:, :half] >> 16) | bits[:, half:]


def _unpack_halves(w):
    lo = pltpu.bitcast(w << 16, F32).astype(BF16)
    hi = pltpu.bitcast(w & jnp.uint32(0xFFFF0000), F32).astype(BF16)
    return jnp.concatenate([lo, hi], axis=1)


def _outproj_kernel(m_ref, wo_ref, x_ref, g_ref, rw_ref, rb_ref, x1_ref, hf_ref, lg_ref):
    x1 = x_ref[...] + jnp.dot(m_ref[...], wo_ref[...], preferred_element_type=F32)
    x1_ref[...] = x1
    ms = jnp.mean(x1 * x1, axis=-1, keepdims=True)
    hf = (x1 * lax.rsqrt(ms + RMS_EPS) * g_ref[...]).astype(BF16)
    hf_ref[...] = _pack_halves(hf)
    lg_ref[...] = jnp.dot(hf, rw_ref[...], preferred_element_type=F32) + rb_ref[...]


def _outproj(merged, wo, x, g, rw, rb, tm=512):
    t, d = x.shape
    ne = rw.shape[1]
    rw_p = jnp.zeros((d, LANES), BF16).at[:, :ne].set(rw.astype(BF16))
    rb_p = jnp.zeros((1, LANES), F32).at[0, :ne].set(rb.astype(F32))
    row = lambda i: (i, 0)
    fixed = lambda i: (0, 0)
    return pl.pallas_call(
        _outproj_kernel,
        out_shape=(jax.ShapeDtypeStruct((t, d), F32),
                   jax.ShapeDtypeStruct((t, d // 2), jnp.uint32),
                   jax.ShapeDtypeStruct((t, LANES), F32)),
        grid=(t // tm,),
        in_specs=[pl.BlockSpec((tm, d), row),
                  pl.BlockSpec((d, d), fixed),
                  pl.BlockSpec((tm, d), row),
                  pl.BlockSpec((1, d), fixed),
                  pl.BlockSpec((d, LANES), fixed),
                  pl.BlockSpec((1, LANES), fixed)],
        out_specs=(pl.BlockSpec((tm, d), row),
                   pl.BlockSpec((tm, d // 2), row),
                   pl.BlockSpec((tm, LANES), row)),
        compiler_params=_cparams(("parallel",)),
        name="outproj_norm_router",
    )(merged, wo, x, g.reshape(1, d).astype(F32), rw_p, rb_p)


def _expert_changed(be_ref, r):
    return jnp.logical_or(r == 0, be_ref[r] != be_ref[jnp.maximum(r - 1, 0)])


def _expert_up_kernel(be_ref, nu_ref, xs_ref, wg_ref, bg_ref, wu_ref, bu_ref, act_ref,
                      wgb_ref, wub_ref):
    r = pl.program_id(1)
    used = r < nu_ref[0]

    @pl.when(jnp.logical_and(used, _expert_changed(be_ref, r)))
    def _():
        wgb_ref[...] = wg_ref[...].astype(BF16)
        wub_ref[...] = wu_ref[...].astype(BF16)

    @pl.when(used)
    def _():
        xs = _unpack_halves(xs_ref[...])
        glu = jnp.minimum(jnp.dot(xs, wgb_ref[...], preferred_element_type=F32) + bg_ref[...],
                          SWIGLU_LIMIT)
        lin = jnp.clip(jnp.dot(xs, wub_ref[...], preferred_element_type=F32) + bu_ref[...],
                       -SWIGLU_LIMIT, SWIGLU_LIMIT)
        act_ref[...] = (glu * _sigmoid(SWIGLU_ALPHA * glu) * (lin + 1.0)).astype(act_ref.dtype)

    @pl.when(jnp.logical_not(used))
    def _():
        act_ref[...] = jnp.zeros_like(act_ref)


def _expert_down_kernel(be_ref, nu_ref, act_ref, wd_ref, bd_ref, y_ref, wdb_ref):
    r = pl.program_id(1)
    used = r < nu_ref[0]

    @pl.when(jnp.logical_and(used, _expert_changed(be_ref, r)))
    def _():
        wdb_ref[...] = wd_ref[...].astype(BF16)

    @pl.when(used)
    def _():
        y_ref[...] = jnp.dot(act_ref[...], wdb_ref[...], preferred_element_type=F32) + bd_ref[...]

    @pl.when(jnp.logical_not(used))
    def _():
        y_ref[...] = jnp.zeros_like(y_ref)


def _experts(xs, block_expert, n_used, wg, bg, wu, bu, wd, bd, tf=1024, tn=2048):
    p = xs.shape[0]
    ne, d, dff = wg.shape
    tf, tn = min(tf, dff), min(tn, d)
    nblk = p // MOE_BLOCK

    def rr(r, nu):
        return jnp.minimum(r, nu[0] - 1)

    wspec = lambda k, t: pl.BlockSpec((None, k, t), lambda c, r, be, nu: (be[rr(r, nu)], 0, c))
    bspec = lambda t: pl.BlockSpec((None, 1, t), lambda c, r, be, nu: (be[rr(r, nu)], 0, c))
    rows = lambda k: pl.BlockSpec((MOE_BLOCK, k), lambda c, r, be, nu: (rr(r, nu), 0))

    act = pl.pallas_call(
        _expert_up_kernel,
        out_shape=jax.ShapeDtypeStruct((p, dff), BF16),
        grid_spec=pltpu.PrefetchScalarGridSpec(
            num_scalar_prefetch=2,
            grid=(dff // tf, nblk),
            in_specs=[rows(d // 2), wspec(d, tf), bspec(tf), wspec(d, tf), bspec(tf)],
            out_specs=pl.BlockSpec((MOE_BLOCK, tf), lambda c, r, be, nu: (r, c)),
            scratch_shapes=[pltpu.VMEM((d, tf), BF16), pltpu.VMEM((d, tf), BF16)]),
        compiler_params=_cparams(("arbitrary", "arbitrary")),
        name="moe_up",
    )(block_expert, n_used, xs, wg, bg.reshape(ne, 1, dff).astype(F32), wu,
      bu.reshape(ne, 1, dff).astype(F32))

    return pl.pallas_call(
        _expert_down_kernel,
        out_shape=jax.ShapeDtypeStruct((p, d), F32),
        grid_spec=pltpu.PrefetchScalarGridSpec(
            num_scalar_prefetch=2,
            grid=(d // tn, nblk),
            in_specs=[rows(dff), wspec(dff, tn), bspec(tn)],
            out_specs=pl.BlockSpec((MOE_BLOCK, tn), lambda c, r, be, nu: (r, c)),
            scratch_shapes=[pltpu.VMEM((dff, tn), BF16)]),
        compiler_params=_cparams(("arbitrary", "arbitrary")),
        name="moe_down",
    )(block_expert, n_used, act, wd, bd.reshape(ne, 1, d).astype(F32))


def _combine_kernel(dest_ref, x1_ref, gw_ref, g_ref, y_hbm, o_ref, *scratch, tt, ct, norm):
    nch = tt // ct
    bufs, sem_ref = scratch[:nch], scratch[nch]
    i = pl.program_id(0)
    last = pl.num_programs(0) - 1
    lead = 2
    grp = 2 * SUBLANES

    def issue_rows(tok0, n, slot, asg0):
        for j in range(n):
            for k in range(MOE_TOPK):
                row = dest_ref[asg0 + j * MOE_TOPK + k]
                pltpu.make_async_copy(y_hbm.at[pl.ds(row, 1), :],
                                      bufs[slot].at[k, pl.ds(tok0 + j, 1), :],
                                      sem_ref.at[slot]).start(priority=k % 2)

    def issue_chunk(step, c):
        asg = (step * tt + c * ct) * MOE_TOPK

        def body(g, _):
            tok0 = pl.multiple_of(g * grp, grp)
            issue_rows(tok0, grp, c, asg + tok0 * MOE_TOPK)
            return 0
        lax.fori_loop(0, ct // grp, body, 0)

    def reduce_group(c, tok0):
        rows = pl.ds(c * ct + tok0, grp)
        acc = x1_ref[rows, :]
        gw = gw_ref[rows, :]
        for k in range(MOE_TOPK):
            acc = acc + gw[:, k:k + 1] * bufs[c][k, pl.ds(tok0, grp), :]
        if norm:
            ms = jnp.mean(acc * acc, axis=-1, keepdims=True)
            acc = acc * lax.rsqrt(ms + RMS_EPS) * g_ref[...]
        o_ref[rows, :] = acc

    @pl.when(i == 0)
    def _():
        for c in range(lead):
            issue_chunk(0, c)

    for c in range(nch):
        for k in range(MOE_TOPK):
            pltpu.make_async_copy(y_hbm.at[pl.ds(0, ct), :], bufs[c].at[k], sem_ref.at[c]).wait()
        c_next = (c + lead) % nch
        step_next = i + (c + lead) // nch

        def reduce_and_issue(g, _, c=c, c_next=c_next, step_next=step_next):
            tok0 = pl.multiple_of(g * grp, grp)
            issue_rows(tok0, grp, c_next, ((step_next * tt + c_next * ct) + tok0) * MOE_TOPK)
            reduce_group(c, tok0)
            return 0

        def reduce_only(g, _, c=c):
            reduce_group(c, pl.multiple_of(g * grp, grp))
            return 0

        if c + lead < nch:
            lax.fori_loop(0, ct // grp, reduce_and_issue, 0)
        else:
            @pl.when(i < last)
            def _(f=reduce_and_issue):
                lax.fori_loop(0, ct // grp, f, 0)

            @pl.when(i == last)
            def _(f=reduce_only):
                lax.fori_loop(0, ct // grp, f, 0)


def _combine(x1, y, dest, gate_w, g, norm, tt=256, ct=64):
    t, d = x1.shape
    nch = tt // ct
    kern = functools.partial(_combine_kernel, tt=tt, ct=ct, norm=norm)
    return pl.pallas_call(
        kern,
        out_shape=jax.ShapeDtypeStruct((t, d), F32),
        grid_spec=pltpu.PrefetchScalarGridSpec(
            num_scalar_prefetch=1,
            grid=(t // tt,),
            in_specs=[pl.BlockSpec((tt, d), lambda i, dref: (i, 0)),
                      pl.BlockSpec((tt, MOE_TOPK), lambda i, dref: (i, 0)),
                      pl.BlockSpec((1, d), lambda i, dref: (0, 0)),
                      pl.BlockSpec(memory_space=pl.ANY)],
            out_specs=pl.BlockSpec((tt, d), lambda i, dref: (i, 0)),
            scratch_shapes=[pltpu.VMEM((MOE_TOPK, ct, d), F32) for _ in range(nch)]
            + [pltpu.SemaphoreType.DMA((nch,))]),
        compiler_params=_cparams(("arbitrary",)),
        name="moe_combine",
    )(dest.reshape(t * MOE_TOPK), x1, gate_w, g.reshape(1, d).astype(F32), y)


def _route_kernel(lg_ref, out_ref, cnt_ref, base_ref, *, n_experts, tm):
    @pl.when(pl.program_id(0) == 0)
    def _():
        base_ref[...] = jnp.zeros_like(base_ref)

    lane = lax.broadcasted_iota(jnp.int32, (tm, LANES), 1)
    lg = jnp.where(lane < n_experts, lg_ref[...], -jnp.inf)
    vals, ids = [], []
    onehot = jnp.zeros((tm, LANES), jnp.bool_)
    for _k in range(MOE_TOPK):
        mx = jnp.max(lg, axis=1, keepdims=True)
        first = jnp.min(jnp.where(lg == mx, lane, LANES), axis=1, keepdims=True)
        pick = lane == first
        onehot = jnp.logical_or(onehot, pick)
        lg = jnp.where(pick, -jnp.inf, lg)
        vals.append(mx)
        ids.append(first)
    exps = [jnp.exp(v - vals[0]) for v in vals]
    denom = exps[0] + exps[1] + exps[2] + exps[3]

    oh = jnp.where(onehot, 1.0, 0.0)
    r_i = lax.broadcasted_iota(jnp.int32, (tm, tm), 0)
    c_i = lax.broadcasted_iota(jnp.int32, (tm, tm), 1)
    earlier = jnp.where(r_i > c_i, 1.0, 0.0).astype(BF16)
    before = base_ref[...] + jnp.dot(earlier, oh.astype(BF16), preferred_element_type=F32)
    base_ref[...] = base_ref[...] + jnp.sum(oh, axis=0, keepdims=True)
    cnt_ref[...] = base_ref[...]

    out = jnp.zeros((tm, LANES), F32)
    for k in range(MOE_TOPK):
        rank = jnp.sum(jnp.where(lane == ids[k], before, 0.0), axis=1, keepdims=True)
        out = jnp.where(lane == k, exps[k] / denom, out)
        out = jnp.where(lane == MOE_TOPK + k, ids[k].astype(F32), out)
        out = jnp.where(lane == 2 * MOE_TOPK + k, rank, out)
    out_ref[...] = out


def _route(logits, n_experts, tm=512):
    t = logits.shape[0]
    kern = functools.partial(_route_kernel, n_experts=n_experts, tm=tm)
    packed, counts = pl.pallas_call(
        kern,
        out_shape=(jax.ShapeDtypeStruct((t, LANES), F32), jax.ShapeDtypeStruct((1, LANES), F32)),
        grid=(t // tm,),
        in_specs=[pl.BlockSpec((tm, LANES), lambda i: (i, 0))],
        out_specs=(pl.BlockSpec((tm, LANES), lambda i: (i, 0)),
                   pl.BlockSpec((1, LANES), lambda i: (0, 0))),
        scratch_shapes=[pltpu.VMEM((1, LANES), F32)],
        compiler_params=_cparams(("arbitrary",)),
        name="moe_route",
    )(logits)
    blk = MOE_BLOCK
    n_asg = t * MOE_TOPK
    p = (-(-n_asg // blk)) * blk + n_experts * blk
    cnt = counts[0, :n_experts].astype(jnp.int32)
    padded = (cnt + blk - 1) // blk * blk
    pad_end = jnp.cumsum(padded)
    pad_start = pad_end - padded
    row_starts = jnp.arange(p // blk, dtype=jnp.int32) * blk
    block_expert = jnp.minimum(jnp.sum(pad_end[None, :] <= row_starts[:, None], axis=1),
                               n_experts - 1).astype(jnp.int32)
    n_used = (pad_end[-1] // blk).astype(jnp.int32).reshape(1)
    gate_w = packed[:, :MOE_TOPK]
    ids = packed[:, MOE_TOPK:2 * MOE_TOPK].astype(jnp.int32)
    rank = packed[:, 2 * MOE_TOPK:3 * MOE_TOPK].astype(jnp.int32)
    sel = ids[:, :, None] == jnp.arange(n_experts, dtype=jnp.int32)[None, None, :]
    dest = rank + jnp.sum(jnp.where(sel, pad_start[None, None, :], 0), axis=-1)
    fill_start = jnp.minimum((pad_start + cnt) // SUBLANES * SUBLANES, p - blk).astype(jnp.int32)
    return gate_w, dest, block_expert, n_used, fill_start, p


def _dispatch_kernel(dest_ref, fill_ref, nu_ref, hf_ref, xs_hbm, zero_ref, sem_ref,
                     *, tt, n_experts, nblk):
    i = pl.program_id(0)

    def zero_fill(start):
        return pltpu.make_async_copy(zero_ref, xs_hbm.at[pl.ds(start, MOE_BLOCK), :],
                                     sem_ref.at[0])

    @pl.when(i == 0)
    def _():
        zero_ref[...] = jnp.zeros_like(zero_ref)
        for e in range(n_experts):
            fill = zero_fill(pl.multiple_of(fill_ref[e], SUBLANES))
            fill.start()
            fill.wait()

        def fill_block(r, _):
            fill = zero_fill(pl.multiple_of(r * MOE_BLOCK, MOE_BLOCK))
            fill.start()
            fill.wait()
            return 0

        lax.fori_loop(nu_ref[0], nblk, fill_block, 0)

    base = i * tt

    def body(tok, _):
        for k in range(MOE_TOPK):
            row = dest_ref[(base + tok) * MOE_TOPK + k]
            pltpu.make_async_copy(hf_ref.at[pl.ds(tok, 1), :],
                                  xs_hbm.at[pl.ds(row, 1), :],
                                  sem_ref.at[1]).start(priority=k % 2)
        return 0

    lax.fori_loop(0, tt, body, 0, unroll=2)
    for k in range(MOE_TOPK):
        pltpu.make_async_copy(hf_ref, xs_hbm.at[pl.ds(0, tt), :], sem_ref.at[1]).wait()


def _dispatch(hfp, dest, fill_start, n_used, p, n_experts, tt=2048):
    t, w = hfp.shape
    tt = min(tt, t)
    kern = functools.partial(_dispatch_kernel, tt=tt, n_experts=n_experts, nblk=p // MOE_BLOCK)
    return pl.pallas_call(
        kern,
        out_shape=jax.ShapeDtypeStruct((p, w), hfp.dtype),
        grid_spec=pltpu.PrefetchScalarGridSpec(
            num_scalar_prefetch=3,
            grid=(t // tt,),
            in_specs=[pl.BlockSpec((tt, w), lambda i, dref, fref, nref: (i, 0))],
            out_specs=pl.BlockSpec(memory_space=pl.ANY),
            scratch_shapes=[pltpu.VMEM((MOE_BLOCK, w), hfp.dtype),
                            pltpu.SemaphoreType.DMA((2,))]),
        compiler_params=_cparams(("arbitrary",)),
        name="moe_dispatch",
    )(dest.reshape(t * MOE_TOPK), fill_start, n_used, hfp)


def _layer(x2, bsz, seq, norm_mix_g, w_in, conv_w, conv_b, lru_wa, lru_ba, lru_wx, lru_bx,
           lru_lambda, w_attn_branch, w_rnn_branch, w_out, norm_ffn_g, router_w, router_b,
           w_gate, b_gate, w_up, b_up, w_down, b_down):
    t, d = x2.shape
    aw = ATTN_HEADS * HEAD_DIM
    r = lru_lambda.shape[0]
    ne = router_w.shape[1]

    hm = _rmsnorm(x2, norm_mix_g, BF16)
    w_in_b = w_in.astype(BF16)
    qkv = _proj(hm, w_in_b, 0, 3 * aw, BF16, "scale_head",
                n_scaled=aw, scale=HEAD_DIM ** -0.5 * LOG2E)
    xr = _proj(hm, w_in_b, 3 * aw, r, F32)
    yr = _proj(hm, w_in_b, 3 * aw + r, r, F32, "gelu")
    gates = _proj(hm, w_in_b, 3 * aw + 2 * r, 2 * d, F32, "sigmoid")

    o = _moba(qkv.reshape(bsz, seq, 3 * aw), bsz, seq).reshape(t, aw)
    hy = _lru(xr.reshape(bsz, seq, r), yr.reshape(bsz, seq, r), conv_w, conv_b,
              lru_wa, lru_ba, lru_wx, lru_bx, lru_lambda, bsz, seq).reshape(t, r)

    merged = _merge(o, hy, w_attn_branch.astype(BF16), w_rnn_branch.astype(BF16), gates)
    x1, hf, logits = _outproj(merged, w_out.astype(BF16), x2, norm_ffn_g, router_w, router_b)

    gate_w, dest, block_expert, n_used, fill_start, p = _route(logits, ne)
    xs = _dispatch(hf, dest, fill_start, n_used, p, ne)
    y = _experts(xs, block_expert, n_used, w_gate, b_gate, w_up, b_up, w_down, b_down)
    return x1, y, dest, gate_w


def kernel(x, norm_mix_g, w_in, conv_w, conv_b, lru_wa, lru_ba, lru_wx, lru_bx, lru_lambda, w_attn_branch, w_rnn_branch, w_out, norm_ffn_g, router_w, router_b, w_gate, b_gate, w_up, b_up, w_down, b_down, norm_final_g):
    bsz, seq, d = x.shape
    depth = w_in.shape[0]
    assert seq % (4 * MOBA_BLOCK) == 0 and seq // MOBA_BLOCK <= LANES and d % LANES == 0
    x2 = x.reshape(bsz * seq, d)
    for l in range(depth):
        x1, y, dest, gate_w = _layer(
            x2, bsz, seq, norm_mix_g[l], w_in[l], conv_w[l], conv_b[l], lru_wa[l], lru_ba[l],
            lru_wx[l], lru_bx[l], lru_lambda[l], w_attn_branch[l], w_rnn_branch[l], w_out[l],
            norm_ffn_g[l], router_w[l], router_b[l], w_gate[l], b_gate[l], w_up[l], b_up[l],
            w_down[l], b_down[l])
        last = l + 1 == depth
        x2 = _combine(x1, y, dest, gate_w, norm_final_g, norm=last)
    return x2.reshape(bsz, seq, d)
```
